```python
import jax, jax.numpy as jnp
from jax import lax
import numpy as np

D_MODEL = 1024
BATCH = 16
SEQ = 4096
DEPTH = 4
DEC_BATCH = 2
DEC_SEQ = 8192
PAST_LEN = 128

N_META = 16
GRID_W = 64
Q_BLOCK = 128
EPS = 1e-6
ROPE_THETA = 10000.0
MLA_HEADS = 8
MLA_NOPE = 64
MLA_ROPE = 32
MLA_QK = MLA_NOPE + MLA_ROPE
MLA_V = 64
Q_LORA = 384
KV_LORA = 256
GQA_HEADS = 8
GQA_KV_HEADS = 2
GQA_HD = 64
AXIAL_HALF = GQA_HD // 2
D_FF = 2688
IN_COLS = Q_LORA + KV_LORA + MLA_ROPE + GQA_HEADS * GQA_HD + 2 * GQA_KV_HEADS * GQA_HD + 2 * D_MODEL

kernel_name = "hybrid_mla_axial_gqa_macaron_encoder"


def rms_norm(x, g):
    x32 = x.astype(jnp.float32)
    y = x32 * lax.rsqrt(jnp.mean(x32 * x32, axis=-1, keepdims=True) + EPS) * g.astype(jnp.float32)
    return y.astype(x.dtype)


def rope_angles(pos, dim):
    inv = 1.0 / (ROPE_THETA ** (jnp.arange(0, dim, 2, dtype=jnp.float32) / dim))
    return pos.astype(jnp.float32)[:, None] * inv[None, :]


def apply_rope(x, ang):
    cos = jnp.cos(ang)[:, None, :].astype(x.dtype)
    sin = jnp.sin(ang)[:, None, :].astype(x.dtype)
    half = x.shape[-1] // 2
    x1, x2 = x[..., :half], x[..., half:]
    return jnp.concatenate([x1 * cos - x2 * sin, x1 * sin + x2 * cos], axis=-1)


def axial_rope(x, ang_row, ang_col):
    return jnp.concatenate([apply_rope(x[..., :AXIAL_HALF], ang_row),
                            apply_rope(x[..., AXIAL_HALF:], ang_col)], axis=-1)


def _attend_block(qb, k, v, scale):
    s = jnp.einsum('bqkgd,bskd->bkgqs', qb, k, preferred_element_type=jnp.float32) * scale
    p = jax.nn.softmax(s, axis=-1).astype(v.dtype)
    return jnp.einsum('bkgqs,bskd->bqkgd', p, v)


def blocked_attention(q, k, v):
    B, L, H, d = q.shape
    Hk = k.shape[2]
    G = H // Hk
    dv = v.shape[-1]
    scale = d ** -0.5
    q = q.reshape(B, L, Hk, G, d)
    o_meta = _attend_block(q[:, :N_META], k, v, scale).reshape(B, N_META, H, dv)
    n_blk = (L - N_META) // Q_BLOCK
    qr = q[:, N_META:].reshape(B, n_blk, Q_BLOCK, Hk, G, d).transpose(1, 0, 2, 3, 4, 5)
    o = lax.map(lambda qb: _attend_block(qb, k, v, scale), qr)
    o = o.transpose(1, 0, 2, 3, 4, 5).reshape(B, L - N_META, H, dv)
    return jnp.concatenate([o_meta, o], axis=1)


def swiglu(h, w13, w2):
    gate, up = jnp.split(h @ w13, 2, axis=-1)
    return (jax.nn.silu(gate) * up) @ w2


def token_mixer(h, ang_1d, ang_row, ang_col, w_in, b_gate, q_a_g, kv_a_g, w_uq, w_ukv,
                mla_qn, mla_kn, gqa_qn, gqa_kn, w_mla_o, w_gqa_o, w_out):
    B, L, _ = h.shape
    cuts = np.cumsum([Q_LORA, KV_LORA, MLA_ROPE, GQA_HEADS * GQA_HD,
                      GQA_KV_HEADS * GQA_HD, GQA_KV_HEADS * GQA_HD]).tolist()
    z = h @ w_in
    c_q, c_kv, k_r, q_g, k_g, v_g, gates = jnp.split(z, cuts, axis=-1)

    q_a = (rms_norm(c_q, q_a_g) @ w_uq).reshape(B, L, MLA_HEADS, MLA_QK)
    kv_a = (rms_norm(c_kv, kv_a_g) @ w_ukv).reshape(B, L, MLA_HEADS, MLA_NOPE + MLA_V)
    q_nope = rms_norm(q_a[..., :MLA_NOPE], mla_qn[:MLA_NOPE])
    q_rope = apply_rope(rms_norm(q_a[..., MLA_NOPE:], mla_qn[MLA_NOPE:]), ang_1d)
    k_nope = rms_norm(kv_a[..., :MLA_NOPE], mla_kn[:MLA_NOPE])
    v_a = kv_a[..., MLA_NOPE:]
    k_rope = apply_rope(rms_norm(k_r, mla_kn[MLA_NOPE:])[:, :, None, :], ang_1d)
    k_rope = jnp.broadcast_to(k_rope, (B, L, MLA_HEADS, MLA_ROPE))
    q_mla = jnp.concatenate([q_nope, q_rope], axis=-1)
    k_mla = jnp.concatenate([k_nope, k_rope], axis=-1)
    o_a = blocked_attention(q_mla, k_mla, v_a).reshape(B, L, MLA_HEADS * MLA_V) @ w_mla_o

    q_b = axial_rope(rms_norm(q_g.reshape(B, L, GQA_HEADS, GQA_HD), gqa_qn), ang_row, ang_col)
    k_b = axial_rope(rms_norm(k_g.reshape(B, L, GQA_KV_HEADS, GQA_HD), gqa_kn), ang_row, ang_col)
    v_b = v_g.reshape(B, L, GQA_KV_HEADS, GQA_HD)
    o_b = blocked_attention(q_b, k_b, v_b).reshape(B, L, GQA_HEADS * GQA_HD) @ w_gqa_o

    g = jax.nn.sigmoid((gates + b_gate).astype(jnp.float32)).astype(h.dtype)
    g_a, g_b = jnp.split(g, 2, axis=-1)
    return (g_a * o_a + g_b * o_b) @ w_out


def trunk(x, meta_tokens, ffn1_norm, ffn1_w13, ffn1_w2, mix_norm, w_in, b_gate, q_a_norm,
          kv_a_norm, w_uq, w_ukv, mla_q_norm, mla_k_norm, gqa_q_norm, gqa_k_norm,
          w_mla_o, w_gqa_o, w_out, ffn2_norm, ffn2_w13, ffn2_w2):
    B, S, _ = x.shape
    rows = S // GRID_W
    zeros_meta = jnp.zeros((N_META,), jnp.int32)
    row_idx = jnp.concatenate([zeros_meta, jnp.repeat(jnp.arange(rows, dtype=jnp.int32), GRID_W)])
    col_idx = jnp.concatenate([zeros_meta, jnp.tile(jnp.arange(GRID_W, dtype=jnp.int32), rows)])
    pos_1d = jnp.arange(N_META + S, dtype=jnp.int32)
    ang_1d = rope_angles(pos_1d, MLA_ROPE)
    ang_row = rope_angles(row_idx, AXIAL_HALF)
    ang_col = rope_angles(col_idx, AXIAL_HALF)

    meta = jnp.broadcast_to(meta_tokens.astype(x.dtype)[None], (B, N_META, D_MODEL))
    h = jnp.concatenate([meta, x], axis=1)
    for l in range(DEPTH):
        h = h + 0.5 * swiglu(rms_norm(h, ffn1_norm[l]), ffn1_w13[l], ffn1_w2[l])
        h = h + token_mixer(rms_norm(h, mix_norm[l]), ang_1d, ang_row, ang_col,
                            w_in[l], b_gate[l], q_a_norm[l], kv_a_norm[l], w_uq[l], w_ukv[l],
                            mla_q_norm[l], mla_k_norm[l], gqa_q_norm[l], gqa_k_norm[l],
                            w_mla_o[l], w_gqa_o[l], w_out[l])
        h = h + 0.5 * swiglu(rms_norm(h, ffn2_norm[l]), ffn2_w13[l], ffn2_w2[l])
    return h[:, N_META:]


def setup_inputs(seed: int = 0) -> dict:
    key = jax.random.key(seed)
    ks = jax.random.split(key, 24)

    def w(k, shape, fan_in):
        return jax.random.normal(k, shape, jnp.float32) * (fan_in ** -0.5)

    def gain(k, shape):
        return 1.0 + 0.02 * jax.random.normal(k, shape, jnp.float32)

    return {
        "x_prompt": jax.random.normal(ks[0], (BATCH, SEQ, D_MODEL), jnp.float32),
        "x_sample": jax.random.normal(ks[1], (DEC_BATCH, DEC_SEQ, D_MODEL), jnp.float32),
        "meta_tokens": jax.random.normal(ks[2], (N_META, D_MODEL), jnp.float32),
        "ffn1_norm": gain(ks[3], (DEPTH, D_MODEL)),
        "ffn1_w13": w(ks[4], (DEPTH, D_MODEL, 2 * D_FF), D_MODEL),
        "ffn1_w2": w(ks[5], (DEPTH, D_FF, D_MODEL), D_FF),
        "mix_norm": gain(ks[6], (DEPTH, D_MODEL)),
        "w_in": w(ks[7], (DEPTH, D_MODEL, IN_COLS), D_MODEL),
        "b_gate": 0.02 * jax.random.normal(ks[8], (DEPTH, 2 * D_MODEL), jnp.float32),
        "q_a_norm": gain(ks[9], (DEPTH, Q_LORA)),
        "kv_a_norm": gain(ks[10], (DEPTH, KV_LORA)),
        "w_uq": w(ks[11], (DEPTH, Q_LORA, MLA_HEADS * MLA_QK), Q_LORA),
        "w_ukv": w(ks[12], (DEPTH, KV_LORA, MLA_HEADS * (MLA_NOPE + MLA_V)), KV_LORA),
        "mla_q_norm": gain(ks[13], (DEPTH, MLA_QK)),
        "mla_k_norm": gain(ks[14], (DEPTH, MLA_QK)),
        "gqa_q_norm": gain(ks[15], (DEPTH, GQA_HD)),
        "gqa_k_norm": gain(ks[16], (DEPTH, GQA_HD)),
        "w_mla_o": w(ks[17], (DEPTH, MLA_HEADS * MLA_V, D_MODEL), MLA_HEADS * MLA_V),
        "w_gqa_o": w(ks[18], (DEPTH, GQA_HEADS * GQA_HD, D_MODEL), GQA_HEADS * GQA_HD),
        "w_out": w(ks[19], (DEPTH, D_MODEL, D_MODEL), D_MODEL),
        "ffn2_norm": gain(ks[20], (DEPTH, D_MODEL)),
        "ffn2_w13": w(ks[21], (DEPTH, D_MODEL, 2 * D_FF), D_MODEL),
        "ffn2_w2": w(ks[22], (DEPTH, D_FF, D_MODEL), D_FF),
    }


def reference(x_prompt, x_sample, meta_tokens, ffn1_norm, ffn1_w13, ffn1_w2, mix_norm, w_in,
              b_gate, q_a_norm, kv_a_norm, w_uq, w_ukv, mla_q_norm, mla_k_norm, gqa_q_norm,
              gqa_k_norm, w_mla_o, w_gqa_o, w_out, ffn2_norm, ffn2_w13, ffn2_w2):
    y_prompt = trunk(x_prompt, meta_tokens, ffn1_norm, ffn1_w13, ffn1_w2, mix_norm, w_in, b_gate,
                     q_a_norm, kv_a_norm, w_uq, w_ukv, mla_q_norm, mla_k_norm, gqa_q_norm,
                     gqa_k_norm, w_mla_o, w_gqa_o, w_out, ffn2_norm, ffn2_w13, ffn2_w2)
    y_sample = trunk(x_sample, meta_tokens, ffn1_norm, ffn1_w13, ffn1_w2, mix_norm, w_in, b_gate,
                     q_a_norm, kv_a_norm, w_uq, w_ukv, mla_q_norm, mla_k_norm, gqa_q_norm,
                     gqa_k_norm, w_mla_o, w_gqa_o, w_out, ffn2_norm, ffn2_w13, ffn2_w2)
    return (y_prompt, y_sample)
```

```python
import functools
import math

import numpy as np
import jax
import jax.numpy as jnp
from jax import lax
from jax.experimental import pallas as pl
from jax.experimental.pallas import tpu as pltpu

N_META = 16
GRID_W = 64
EPS = 1e-6
ROPE_THETA = 10000.0
MLA_HEADS = 8
MLA_NOPE = 64
MLA_ROPE = 32
MLA_QK = MLA_NOPE + MLA_ROPE
MLA_V = 64
Q_LORA = 384
KV_LORA = 256
GQA_HEADS = 8
GQA_KV_HEADS = 2
GQA_GROUP = GQA_HEADS // GQA_KV_HEADS
GQA_HD = 64
AXIAL_HALF = GQA_HD // 2
ROPE_PAIR = 16

LANE = 128
MXU_DIM = 256
VMEM_LIMIT = 56 * 1024 * 1024

META_PAD = LANE
LOG2E = math.log2(math.e)
NEG_BIG = -1e30

BF16 = jnp.bfloat16
F32 = jnp.float32


def _div_tile(n, target, mult):
    best = None
    for t in range(mult, min(n, target) + 1, mult):
        if n % t == 0:
            best = t
    assert best is not None, (n, target, mult)
    return best


def _ff_chunks(d_ff):
    chunks, c0 = [], 0
    while c0 < d_ff:
        w = min(3 * MXU_DIM, d_ff - c0)
        chunks.append((c0, w))
        c0 += w
    return tuple(chunks)


def _dot(a, b):
    return jnp.dot(a, b, preferred_element_type=F32)


def _rms(x, g):
    return x * lax.rsqrt(jnp.mean(x * x, axis=-1, keepdims=True) + EPS) * g


def _const_spec(shape):
    nd = len(shape)
    return pl.BlockSpec(shape, lambda *_: (0,) * nd, pipeline_mode=pl.Buffered(1))


def _params(*sem):
    return pltpu.CompilerParams(dimension_semantics=sem, vmem_limit_bytes=VMEM_LIMIT)


def _ffn_kernel(h_ref, g_ref, w13_ref, w2_ref, o_ref, *, chunks):
    h = h_ref[...]
    xn = _rms(h, g_ref[...]).astype(BF16)
    acc = None
    off = 0
    for c0, w in chunks:
        gu = _dot(xn, w13_ref[:, off:off + 2 * w])
        gate, up = gu[:, :w], gu[:, w:]
        a = (gate * jax.nn.sigmoid(gate) * up).astype(BF16)
        part = _dot(a, w2_ref[c0:c0 + w, :])
        acc = part if acc is None else acc + part
        off += 2 * w
    o_ref[...] = h + 0.5 * acc


def _ffn(h, g, w13r, w2, chunks):
    rows, d = h.shape
    tm = _div_tile(rows, 640, LANE)
    return pl.pallas_call(
        functools.partial(_ffn_kernel, chunks=chunks),
        grid=(rows // tm,),
        in_specs=[
            pl.BlockSpec((tm, d), lambda i: (i, 0)),
            _const_spec(g.shape),
            _const_spec(w13r.shape),
            _const_spec(w2.shape),
        ],
        out_specs=pl.BlockSpec((tm, d), lambda i: (i, 0)),
        out_shape=jax.ShapeDtypeStruct((rows, d), F32),
        compiler_params=_params("parallel"),
        name="ffn",
    )(h, g, w13r, w2)


_C_QG = 0
_C_KR = 1024
_C_KG = 1152
_C_VG = 1280
_C_CQ = 1408
_C_CKV = 1792
_C_END = 2048


def _group_ms(x, bd):
    x2 = x * x
    hi = x2.astype(BF16)
    lo = (x2 - hi.astype(F32)).astype(BF16)
    return _dot(hi, bd) + _dot(lo, bd)


def _rope(y, cos, sin, first_half):
    partner = jnp.where(first_half, pltpu.roll(y, LANE - ROPE_PAIR, 1), pltpu.roll(y, ROPE_PAIR, 1))
    return y * cos + partner * sin


def _proj_kernel(h_ref, gmix_ref, win_ref, gq_ref, gkv_ref, wuq_ref, wukv_ref,
                 bd_a_ref, bd_c_ref, bd_m_ref, gaq_ref, gak_ref, gakr_ref, gbq_ref, gbk_ref,
                 cos_a_ref, sin_a_ref, cos_b_ref, sin_b_ref,
                 qa_ref, ka_ref, va_ref, qb_ref, kb_ref, vb_ref):
    tm = h_ref.shape[0]
    hn = _rms(h_ref[...], gmix_ref[...]).astype(BF16)
    z = _dot(hn, win_ref[...])
    cq = _rms(z[:, _C_CQ:_C_CKV], gq_ref[...]).astype(BF16)
    ckv = _rms(z[:, _C_CKV:_C_END], gkv_ref[...]).astype(BF16)
    qa = _dot(cq, wuq_ref[...])
    kva = _dot(ckv, wukv_ref[...])

    lane = lax.broadcasted_iota(jnp.int32, (tm, LANE), 1)
    first_half = (lane % (2 * ROPE_PAIR)) < ROPE_PAIR
    cos_a, sin_a = cos_a_ref[...], sin_a_ref[...]
    cos_b, sin_b = cos_b_ref[...], sin_b_ref[...]
    bd_a, bd_c, bd_m = bd_a_ref[...], bd_c_ref[...], bd_m_ref[...]

    x = z[:, _C_KR:_C_VG]
    y = x * lax.rsqrt(_group_ms(x, bd_m) + EPS)
    kr = _rope(y[:, :LANE] * gakr_ref[...], cos_a, sin_a, first_half)
    kb_ref[...] = _rope(y[:, LANE:] * gbk_ref[...], cos_b, sin_b, first_half).astype(BF16)
    vb_ref[...] = z[:, _C_VG:_C_CQ].astype(BF16)
    va_ref[...] = kva[:, MLA_HEADS * LANE:].astype(BF16)

    for c in range(MLA_HEADS * LANE // MXU_DIM):
        sl = slice(c * MXU_DIM, (c + 1) * MXU_DIM)
        x = qa[:, sl]
        y = x * lax.rsqrt(_group_ms(x, bd_a) + EPS) * gaq_ref[:, sl]
        xk = kva[:, sl]
        yk = xk * lax.rsqrt(_group_ms(xk, bd_c) + EPS) * gak_ref[:, sl]
        xg = z[:, sl]
        yg = xg * lax.rsqrt(_group_ms(xg, bd_c) + EPS) * gbq_ref[:, sl]
        for half in range(2):
            hs = slice(half * LANE, (half + 1) * LANE)
            head = 2 * c + half
            qa_ref[head] = _rope(y[:, hs], cos_a, sin_a, first_half).astype(BF16)
            ka_ref[head] = (yk[:, hs] + kr).astype(BF16)
            qb_ref[head] = _rope(yg[:, hs], cos_b, sin_b, first_half).astype(BF16)


def _proj(h, lw, tabs, seq_rows):
    rows, d = h.shape
    tm = _div_tile(seq_rows, 640, LANE)
    per_seq = seq_rows // tm
    row_spec = lambda w: pl.BlockSpec((tm, w), lambda i: (i, 0))
    tab_spec = pl.BlockSpec((tm, LANE), lambda i: (i % per_seq, 0))
    head_spec = pl.BlockSpec((MLA_HEADS, tm, LANE), lambda i: (0, i, 0))
    consts = [lw["g_mix"], lw["w_in_s"], lw["g_q"], lw["g_kv"], lw["w_uq_p"], lw["w_ukv_p"],
              tabs["bd_a"], tabs["bd_c"], tabs["bd_m"],
              lw["ga_q"], lw["ga_k"], lw["ga_kr"], lw["gb_q"], lw["gb_k"]]
    heads = jax.ShapeDtypeStruct((MLA_HEADS, rows, LANE), BF16)
    return pl.pallas_call(
        _proj_kernel,
        grid=(rows // tm,),
        in_specs=[row_spec(d)] + [_const_spec(c.shape) for c in consts] + [tab_spec] * 4,
        out_specs=[head_spec, head_spec, row_spec(MLA_HEADS * MLA_V), head_spec,
                   row_spec(LANE), row_spec(LANE)],
        out_shape=[heads, heads, jax.ShapeDtypeStruct((rows, MLA_HEADS * MLA_V), BF16), heads,
                   jax.ShapeDtypeStruct((rows, LANE), BF16), jax.ShapeDtypeStruct((rows, LANE), BF16)],
        compiler_params=_params("parallel"),
        name="proj",
    )(h, *consts, tabs["cos_a"], tabs["sin_a"], tabs["cos_b"], tabs["sin_b"])


def _flash_step(q, k, v, carry, mask):
    m, l, acc = carry
    s = lax.dot_general(q, k, (((1,), (1,)), ((), ())), preferred_element_type=F32)
    if mask is not None:
        s = jnp.where(mask, s, NEG_BIG)
    m_new = jnp.maximum(m, jnp.max(s, axis=-1, keepdims=True))
    p = jnp.exp2(s - m_new)
    alpha = jnp.exp2(m - m_new)
    l = alpha * l + jnp.sum(p, axis=-1, keepdims=True)
    acc = alpha * acc + _dot(p.astype(BF16), v)
    return m_new, l, acc


def _flash(qs, k_at, v_at, n_real, tk):
    n = len(qs)
    init = tuple((jnp.full((q.shape[0], 1), NEG_BIG, F32), jnp.zeros((q.shape[0], 1), F32),
                  jnp.zeros((q.shape[0], LANE), F32)) for q in qs)

    def body(j, carries):
        start = pl.multiple_of(j * tk, tk)
        return tuple(_flash_step(qs[i], k_at(i, start, tk), v_at(i, start, tk), carries[i], None)
                     for i in range(n))

    carries = lax.fori_loop(0, n_real // tk, body, init)
    meta_mask = lax.broadcasted_iota(jnp.int32, (1, META_PAD), 1) < N_META
    outs = []
    for i in range(n):
        _, l, acc = _flash_step(qs[i], k_at(i, n_real, META_PAD), v_at(i, n_real, META_PAD),
                                carries[i], meta_mask)
        outs.append(acc / l)
    return outs


def _mla_kernel(q_ref, k_ref, v_ref, o_ref, *, n_real, tk):
    qs = [q_ref[0, 0], q_ref[1, 0]]
    k_at = lambda i, s, n: k_ref[i, 0, pl.ds(s, n), :]
    v_at = lambda i, s, n: v_ref[0, pl.ds(s, n), :]
    o0, o1 = _flash(qs, k_at, v_at, n_real, tk)
    lane = lax.broadcasted_iota(jnp.int32, o0.shape, 1)
    o_ref[0] = jnp.where(lane < MLA_V, o0, o1).astype(BF16)


def _mla_attn(qa, ka, va, batch, seq_rows):
    n_real = seq_rows - META_PAD
    tq = _div_tile(seq_rows, 640, 16)
    tk = _div_tile(n_real, 512, LANE)
    q4 = qa.reshape(MLA_HEADS, batch, seq_rows, LANE)
    k4 = ka.reshape(MLA_HEADS, batch, seq_rows, LANE)
    v3 = va.reshape(batch, seq_rows, MLA_HEADS * MLA_V)
    return pl.pallas_call(
        functools.partial(_mla_kernel, n_real=n_real, tk=tk),
        grid=(batch, MLA_HEADS // 2, seq_rows // tq),
        in_specs=[
            pl.BlockSpec((2, 1, tq, LANE), lambda b, j, i: (j, b, i, 0)),
            pl.BlockSpec((2, 1, seq_rows, LANE), lambda b, j, i: (j, b, 0, 0)),
            pl.BlockSpec((1, seq_rows, LANE), lambda b, j, i: (b, 0, j)),
        ],
        out_specs=pl.BlockSpec((1, tq, LANE), lambda b, j, i: (b, i, j)),
        out_shape=jax.ShapeDtypeStruct((batch, seq_rows, MLA_HEADS * MLA_V), BF16),
        compiler_params=_params("parallel", "parallel", "parallel"),
        name="mla_attn",
    )(q4, k4, v3).reshape(batch * seq_rows, MLA_HEADS * MLA_V)


def _gqa_kernel(q_ref, k_ref, v_ref, o_ref, *, n_real, tk):
    g, _, tq, _ = q_ref.shape
    q = q_ref[:, 0].reshape(g * tq, LANE)
    k_at = lambda i, s, n: k_ref[0, pl.ds(s, n), :]
    v_at = lambda i, s, n: v_ref[0, pl.ds(s, n), :]
    (o,) = _flash([q], k_at, v_at, n_real, tk)
    first_kv = pl.program_id(1) == 0
    lane = lax.broadcasted_iota(jnp.int32, (tq, LANE), 1)
    rolled = pltpu.roll(o, GQA_HD, 1)
    for pair in range(g // 2):
        lo = slice((2 * pair) * tq, (2 * pair + 1) * tq)
        hi = slice((2 * pair + 1) * tq, (2 * pair + 2) * tq)
        a_lo = jnp.where(first_kv, o[lo], rolled[lo])
        a_hi = jnp.where(first_kv, rolled[hi], o[hi])
        o_ref[0, :, pair * LANE:(pair + 1) * LANE] = jnp.where(lane < GQA_HD, a_lo, a_hi).astype(BF16)


def _gqa_attn(qb, kb, vb, batch, seq_rows):
    n_real = seq_rows - META_PAD
    tq = _div_tile(seq_rows, 208, 16)
    tk = _div_tile(n_real, 512, LANE)
    q4 = qb.reshape(GQA_HEADS, batch, seq_rows, LANE)
    k3 = kb.reshape(batch, seq_rows, LANE)
    v3 = vb.reshape(batch, seq_rows, LANE)
    width = GQA_GROUP * GQA_HD
    return pl.pallas_call(
        functools.partial(_gqa_kernel, n_real=n_real, tk=tk),
        grid=(batch, GQA_KV_HEADS, seq_rows // tq),
        in_specs=[
            pl.BlockSpec((GQA_GROUP, 1, tq, LANE), lambda b, j, i: (j, b, i, 0)),
            pl.BlockSpec((1, seq_rows, LANE), lambda b, j, i: (b, 0, 0)),
            pl.BlockSpec((1, seq_rows, LANE), lambda b, j, i: (b, 0, 0)),
        ],
        out_specs=pl.BlockSpec((1, tq, width), lambda b, j, i: (b, i, j)),
        out_shape=jax.ShapeDtypeStruct((batch, seq_rows, GQA_HEADS * GQA_HD), BF16),
        compiler_params=_params("parallel", "parallel", "parallel"),
        name="gqa_attn",
    )(q4, k3, v3).reshape(batch * seq_rows, GQA_HEADS * GQA_HD)


def _merge_kernel(h_ref, oa_ref, ob_ref, gmix_ref, wg_ref, bg_ref, wao_ref, wbo_ref, wout_ref, o_ref):
    h = h_ref[...]
    d = h.shape[-1]
    hn = _rms(h, gmix_ref[...]).astype(BF16)
    gates = jax.nn.sigmoid(_dot(hn, wg_ref[...]) + bg_ref[...])
    merged = gates[:, :d] * _dot(oa_ref[...], wao_ref[...]) + gates[:, d:] * _dot(ob_ref[...], wbo_ref[...])
    o_ref[...] = h + _dot(merged.astype(BF16), wout_ref[...])


def _merge(h, oa, ob, lw):
    rows, d = h.shape
    tm = _div_tile(rows, 640, LANE)
    row_spec = lambda w: pl.BlockSpec((tm, w), lambda i: (i, 0))
    consts = [lw["g_mix"], lw["w_gates"], lw["b_gate"], lw["w_mla_o"], lw["w_gqa_o"], lw["w_out"]]
    return pl.pallas_call(
        _merge_kernel,
        grid=(rows // tm,),
        in_specs=[row_spec(d), row_spec(oa.shape[1]), row_spec(ob.shape[1])]
        + [_const_spec(c.shape) for c in consts],
        out_specs=row_spec(d),
        out_shape=jax.ShapeDtypeStruct((rows, d), F32),
        compiler_params=_params("parallel"),
        name="merge",
    )(h, oa, ob, *consts)


def _block_diag_mean(groups):
    assert sum(groups) == MXU_DIM
    m = np.zeros((MXU_DIM, MXU_DIM), np.float32)
    c0 = 0
    for gsz in groups:
        m[c0:c0 + gsz, c0:c0 + gsz] = 1.0 / gsz
        c0 += gsz
    return jnp.asarray(m, BF16)


def _rope_angles(pos, dim):
    inv = 1.0 / (ROPE_THETA ** (jnp.arange(0, dim, 2, dtype=jnp.float32) / dim))
    return pos.astype(jnp.float32)[:, None] * inv[None, :]


def _tables(n_real):
    p = jnp.arange(n_real + META_PAD, dtype=jnp.int32)
    real = p < n_real
    meta = (p >= n_real) & (p < n_real + N_META)
    pos_1d = jnp.where(real, p + N_META, jnp.where(meta, p - n_real, 0))
    row = jnp.where(real, p // GRID_W, 0)
    col = jnp.where(real, p % GRID_W, 0)
    a1, ar, ac = _rope_angles(pos_1d, MLA_ROPE), _rope_angles(row, AXIAL_HALF), _rope_angles(col, AXIAL_HALF)
    ones = jnp.ones((p.shape[0], MLA_NOPE), F32)
    pad1 = jnp.ones((p.shape[0], LANE - MLA_QK), F32)
    cos_a = jnp.concatenate([ones, jnp.cos(a1), jnp.cos(a1), pad1], axis=1)
    sin_a = jnp.concatenate([0 * ones, -jnp.sin(a1), jnp.sin(a1), 0 * pad1], axis=1)
    cos_b = jnp.tile(jnp.concatenate([jnp.cos(ar), jnp.cos(ar), jnp.cos(ac), jnp.cos(ac)], axis=1), (1, 2))
    sin_b = jnp.tile(jnp.concatenate([-jnp.sin(ar), jnp.sin(ar), -jnp.sin(ac), jnp.sin(ac)], axis=1), (1, 2))
    return dict(
        cos_a=cos_a, sin_a=sin_a, cos_b=cos_b, sin_b=sin_b,
        bd_a=_block_diag_mean([MLA_NOPE, MLA_ROPE, LANE - MLA_QK] * 2),
        bd_c=_block_diag_mean([GQA_HD] * 4),
        bd_m=_block_diag_mean([MLA_NOPE, MLA_ROPE, LANE - MLA_QK, GQA_HD, GQA_HD]),
    )


def _prep_layer(ffn1_norm, ffn1_w13, ffn1_w2, mix_norm, w_in, b_gate, q_a_norm, kv_a_norm, w_uq, w_ukv,
                mla_q_norm, mla_k_norm, gqa_q_norm, gqa_k_norm, w_mla_o, w_gqa_o, w_out,
                ffn2_norm, ffn2_w13, ffn2_w2, chunks):
    d = w_in.shape[0]
    d_ff = ffn1_w2.shape[0]

    def w13r(w13):
        pieces = []
        for c0, w in chunks:
            pieces += [w13[:, c0:c0 + w], w13[:, d_ff + c0:d_ff + c0 + w]]
        return jnp.concatenate(pieces, axis=1).astype(BF16)

    cuts = np.cumsum([0, Q_LORA, KV_LORA, MLA_ROPE, GQA_HEADS * GQA_HD, GQA_KV_HEADS * GQA_HD,
                      GQA_KV_HEADS * GQA_HD]).tolist()
    w_cq, w_ckv, w_kr, w_qg, w_kg, w_vg = (w_in[:, cuts[i]:cuts[i + 1]] for i in range(6))
    w_gates = w_in[:, cuts[6]:]
    z64 = jnp.zeros((d, GQA_HD), F32)
    qg_cols = []
    for hd in range(GQA_HEADS):
        wq = w_qg[:, hd * GQA_HD:(hd + 1) * GQA_HD]
        qg_cols += [wq, z64] if hd // GQA_GROUP == 0 else [z64, wq]
    kr_cols = [jnp.zeros((d, MLA_NOPE), F32), w_kr, jnp.zeros((d, LANE - MLA_QK), F32)]
    w_in_s = jnp.concatenate(qg_cols + kr_cols + [w_kg, w_vg, w_cq, w_ckv], axis=1).astype(BF16)
    assert w_in_s.shape[1] == _C_END

    uq_cols, ukv_cols, v_cols = [], [], []
    for hd in range(MLA_HEADS):
        uq_cols += [w_uq[:, hd * MLA_QK:(hd + 1) * MLA_QK], jnp.zeros((Q_LORA, LANE - MLA_QK), F32)]
        base = hd * (MLA_NOPE + MLA_V)
        ukv_cols += [w_ukv[:, base:base + MLA_NOPE], jnp.zeros((KV_LORA, LANE - MLA_NOPE), F32)]
        v_cols += [w_ukv[:, base + MLA_NOPE:base + MLA_NOPE + MLA_V]]
    w_uq_p = jnp.concatenate(uq_cols, axis=1).astype(BF16)
    w_ukv_p = jnp.concatenate(ukv_cols + v_cols, axis=1).astype(BF16)

    zq = jnp.zeros((LANE - MLA_QK,), F32)
    ga_q = jnp.tile(jnp.concatenate([mla_q_norm, zq]), MLA_HEADS) * (MLA_QK ** -0.5 * LOG2E)
    ga_k = jnp.tile(jnp.concatenate([mla_k_norm[:MLA_NOPE], jnp.zeros((LANE - MLA_NOPE,), F32)]), MLA_HEADS)
    ga_kr = jnp.concatenate([jnp.zeros((MLA_NOPE,), F32), mla_k_norm[MLA_NOPE:], zq])
    z1 = jnp.zeros((GQA_HD,), F32)
    gb_q = jnp.concatenate([jnp.concatenate([gqa_q_norm, z1] if hd // GQA_GROUP == 0 else [z1, gqa_q_norm])
                            for hd in range(GQA_HEADS)]) * (GQA_HD ** -0.5 * LOG2E)
    gb_k = jnp.tile(gqa_k_norm, GQA_KV_HEADS)
    row = lambda v: v.reshape(1, -1).astype(F32)
    return dict(
        g_ffn1=row(ffn1_norm), w13r_1=w13r(ffn1_w13), w2_1=ffn1_w2.astype(BF16),
        g_ffn2=row(ffn2_norm), w13r_2=w13r(ffn2_w13), w2_2=ffn2_w2.astype(BF16),
        g_mix=row(mix_norm), w_in_s=w_in_s, g_q=row(q_a_norm), g_kv=row(kv_a_norm),
        w_uq_p=w_uq_p, w_ukv_p=w_ukv_p,
        ga_q=row(ga_q), ga_k=row(ga_k), ga_kr=row(ga_kr), gb_q=row(gb_q), gb_k=row(gb_k),
        w_gates=w_gates.astype(BF16), b_gate=row(b_gate),
        w_mla_o=w_mla_o.astype(BF16), w_gqa_o=w_gqa_o.astype(BF16), w_out=w_out.astype(BF16),
    )


def _trunk(x, meta_tokens, layers, chunks):
    batch, n_real, d = x.shape
    assert n_real % GRID_W == 0 and n_real % LANE == 0
    seq_rows = n_real + META_PAD
    tail = jnp.concatenate([meta_tokens.astype(x.dtype), jnp.zeros((META_PAD - N_META, d), x.dtype)], axis=0)
    h = jnp.concatenate([x, jnp.broadcast_to(tail[None], (batch, META_PAD, d))], axis=1)
    h = h.reshape(batch * seq_rows, d)
    tabs = _tables(n_real)
    for lw in layers:
        h = _ffn(h, lw["g_ffn1"], lw["w13r_1"], lw["w2_1"], chunks)
        qa, ka, va, qb, kb, vb = _proj(h, lw, tabs, seq_rows)
        oa = _mla_attn(qa, ka, va, batch, seq_rows)
        ob = _gqa_attn(qb, kb, vb, batch, seq_rows)
        h = _merge(h, oa, ob, lw)
        h = _ffn(h, lw["g_ffn2"], lw["w13r_2"], lw["w2_2"], chunks)
    return h.reshape(batch, seq_rows, d)[:, :n_real]


def kernel(x_prompt, x_sample, meta_tokens, ffn1_norm, ffn1_w13, ffn1_w2, mix_norm, w_in, b_gate, q_a_norm,
           kv_a_norm, w_uq, w_ukv, mla_q_norm, mla_k_norm, gqa_q_norm, gqa_k_norm, w_mla_o, w_gqa_o, w_out,
           ffn2_norm, ffn2_w13, ffn2_w2):
    per_layer = (ffn1_norm, ffn1_w13, ffn1_w2, mix_norm, w_in, b_gate, q_a_norm, kv_a_norm, w_uq, w_ukv,
                 mla_q_norm, mla_k_norm, gqa_q_norm, gqa_k_norm, w_mla_o, w_gqa_o, w_out,
                 ffn2_norm, ffn2_w13, ffn2_w2)
    chunks = _ff_chunks(ffn1_w2.shape[1])
    layers = [_prep_layer(*(p[l] for p in per_layer), chunks=chunks) for l in range(ffn1_norm.shape[0])]
    return (_trunk(x_prompt, meta_tokens, layers, chunks), _trunk(x_sample, meta_tokens, layers, chunks))
```

```python
import functools
import math

import numpy as np
import jax
import jax.numpy as jnp
from jax import lax
from jax.experimental import pallas as pl
from jax.experimental.pallas import tpu as pltpu

N_META = 16
GRID_W = 64
EPS = 1e-6
ROPE_THETA = 10000.0
MLA_HEADS = 8
MLA_NOPE = 64
MLA_ROPE = 32
MLA_QK = MLA_NOPE + MLA_ROPE
MLA_V = 64
Q_LORA = 384
KV_LORA = 256
GQA_HEADS = 8
GQA_KV_HEADS = 2
GQA_GROUP = GQA_HEADS // GQA_KV_HEADS
GQA_HD = 64
AXIAL_HALF = GQA_HD // 2
ROPE_PAIR = 16
HEAD_V = 64

LANE = 128
MXU_DIM = 256
BF16_ROWS = 16
VMEM_LIMIT = 56 * 1024 * 1024

META_PAD = LANE
VT_ROWS = HEAD_V + BF16_ROWS
KV_BLOCK = 256
LOG2E = math.log2(math.e)
NEG_BIG = -1e30

BF16 = jnp.bfloat16
F32 = jnp.float32


def _div_tile(n, target, mult):
    best = None
    for t in range(mult, min(n, target) + 1, mult):
        if n % t == 0:
            best = t
    assert best is not None, (n, target, mult)
    return best


def _ff_chunks(d_ff):
    chunks, c0 = [], 0
    while c0 < d_ff:
        w = min(3 * MXU_DIM, d_ff - c0)
        chunks.append((c0, w))
        c0 += w
    return tuple(chunks)


def _dot(a, b):
    return jnp.dot(a, b, preferred_element_type=F32)


def _rms(x, g):
    return x * lax.rsqrt(jnp.mean(x * x, axis=-1, keepdims=True) + EPS) * g


def _const_spec(shape):
    nd = len(shape)
    return pl.BlockSpec(shape, lambda *_: (0,) * nd, pipeline_mode=pl.Buffered(1))


def _params(*sem):
    return pltpu.CompilerParams(dimension_semantics=sem, vmem_limit_bytes=VMEM_LIMIT)


def _ffn_kernel(h_ref, g_ref, w13_ref, w2_ref, o_ref, *, chunks):
    h = h_ref[...]
    xn = _rms(h, g_ref[...]).astype(BF16)
    acc = None
    off = 0
    for c0, w in chunks:
        gu = _dot(xn, w13_ref[:, off:off + 2 * w])
        gate, up = gu[:, :w], gu[:, w:]
        a = (gate * jax.nn.sigmoid(gate) * up).astype(BF16)
        part = _dot(a, w2_ref[c0:c0 + w, :])
        acc = part if acc is None else acc + part
        off += 2 * w
    o_ref[...] = h + 0.5 * acc


def _ffn(h, g, w13r, w2, chunks):
    rows, d = h.shape
    tm = _div_tile(rows, 640, LANE)
    return pl.pallas_call(
        functools.partial(_ffn_kernel, chunks=chunks),
        grid=(rows // tm,),
        in_specs=[
            pl.BlockSpec((tm, d), lambda i: (i, 0)),
            _const_spec(g.shape),
            _const_spec(w13r.shape),
            _const_spec(w2.shape),
        ],
        out_specs=pl.BlockSpec((tm, d), lambda i: (i, 0)),
        out_shape=jax.ShapeDtypeStruct((rows, d), F32),
        compiler_params=_params("parallel"),
        name="ffn",
    )(h, g, w13r, w2)


_C_QG = 0
_C_KR = 1024
_C_KG = 1152
_C_VG = 1280
_C_CQ = 1408
_C_CKV = 1792
_C_END = 2048


def _group_ms(x, bd):
    x2 = x * x
    hi = x2.astype(BF16)
    lo = (x2 - hi.astype(F32)).astype(BF16)
    return _dot(hi, bd) + _dot(lo, bd)


def _rope(y, cos, sin, first_half):
    partner = jnp.where(first_half, pltpu.roll(y, LANE - ROPE_PAIR, 1), pltpu.roll(y, ROPE_PAIR, 1))
    return y * cos + partner * sin


def _store_vt(vt_ref, head0, v_pair, ones_tile):
    vt = v_pair.T
    for k in range(2):
        vt_ref[head0 + k, :HEAD_V, :] = vt[k * HEAD_V:(k + 1) * HEAD_V].astype(BF16)
        vt_ref[head0 + k, HEAD_V:, :] = ones_tile


def _proj_kernel(h_ref, gmix_ref, win_ref, gq_ref, gkv_ref, wuq_ref, wukv_ref,
                 bd_a_ref, bd_c_ref, bd_m_ref, gaq_ref, gak_ref, gakr_ref, gbq_ref, gbk_ref,
                 cos_a_ref, sin_a_ref, cos_b_ref, sin_b_ref,
                 qa_ref, ka_ref, vta_ref, qb_ref, kb_ref, vtb_ref):
    tm = h_ref.shape[0]
    hn = _rms(h_ref[...], gmix_ref[...]).astype(BF16)
    z = _dot(hn, win_ref[...])
    cq = _rms(z[:, _C_CQ:_C_CKV], gq_ref[...]).astype(BF16)
    ckv = _rms(z[:, _C_CKV:_C_END], gkv_ref[...]).astype(BF16)
    qa = _dot(cq, wuq_ref[...])
    kva = _dot(ckv, wukv_ref[...])

    lane = lax.broadcasted_iota(jnp.int32, (tm, LANE), 1)
    first_half = (lane % (2 * ROPE_PAIR)) < ROPE_PAIR
    cos_a, sin_a = cos_a_ref[...], sin_a_ref[...]
    cos_b, sin_b = cos_b_ref[...], sin_b_ref[...]
    bd_a, bd_c, bd_m = bd_a_ref[...], bd_c_ref[...], bd_m_ref[...]
    ones_tile = (lax.broadcasted_iota(jnp.int32, (BF16_ROWS, tm), 0) == 0).astype(F32).astype(BF16)

    x = z[:, _C_KR:_C_VG]
    y = x * lax.rsqrt(_group_ms(x, bd_m) + EPS)
    kr = _rope(y[:, :LANE] * gakr_ref[...], cos_a, sin_a, first_half)
    kb_ref[...] = _rope(y[:, LANE:] * gbk_ref[...], cos_b, sin_b, first_half).astype(BF16)
    _store_vt(vtb_ref, 0, z[:, _C_VG:_C_CQ], ones_tile)
    for c in range(MLA_HEADS // 2):
        v0 = MLA_HEADS * LANE + c * LANE
        _store_vt(vta_ref, 2 * c, kva[:, v0:v0 + LANE], ones_tile)

    for c in range(MLA_HEADS * LANE // MXU_DIM):
        sl = slice(c * MXU_DIM, (c + 1) * MXU_DIM)
        x = qa[:, sl]
        y = x * lax.rsqrt(_group_ms(x, bd_a) + EPS) * gaq_ref[:, sl]
        xk = kva[:, sl]
        yk = xk * lax.rsqrt(_group_ms(xk, bd_c) + EPS) * gak_ref[:, sl]
        xg = z[:, sl]
        yg = xg * lax.rsqrt(_group_ms(xg, bd_c) + EPS) * gbq_ref[:, sl]
        for half in range(2):
            hs = slice(half * LANE, (half + 1) * LANE)
            head = 2 * c + half
            qa_ref[head] = _rope(y[:, hs], cos_a, sin_a, first_half).astype(BF16)
            ka_ref[head] = (yk[:, hs] + kr).astype(BF16)
            qb_ref[head] = _rope(yg[:, hs], cos_b, sin_b, first_half).astype(BF16)


def _proj(h, lw, tabs, seq_rows):
    rows, d = h.shape
    tm = _div_tile(seq_rows, 640, LANE)
    per_seq = seq_rows // tm
    row_spec = lambda w: pl.BlockSpec((tm, w), lambda i: (i, 0))
    tab_spec = pl.BlockSpec((tm, LANE), lambda i: (i % per_seq, 0))
    head_spec = pl.BlockSpec((MLA_HEADS, tm, LANE), lambda i: (0, i, 0))
    vt_spec = lambda n: pl.BlockSpec((n, VT_ROWS, tm), lambda i: (0, 0, i))
    consts = [lw["g_mix"], lw["w_in_s"], lw["g_q"], lw["g_kv"], lw["w_uq_p"], lw["w_ukv_p"],
              tabs["bd_a"], tabs["bd_c"], tabs["bd_m"],
              lw["ga_q"], lw["ga_k"], lw["ga_kr"], lw["gb_q"], lw["gb_k"]]
    heads = jax.ShapeDtypeStruct((MLA_HEADS, rows, LANE), BF16)
    return pl.pallas_call(
        _proj_kernel,
        grid=(rows // tm,),
        in_specs=[row_spec(d)] + [_const_spec(c.shape) for c in consts] + [tab_spec] * 4,
        out_specs=[head_spec, head_spec, vt_spec(MLA_HEADS), head_spec, row_spec(LANE),
                   vt_spec(GQA_KV_HEADS)],
        out_shape=[heads, heads, jax.ShapeDtypeStruct((MLA_HEADS, VT_ROWS, rows), BF16), heads,
                   jax.ShapeDtypeStruct((rows, LANE), BF16),
                   jax.ShapeDtypeStruct((GQA_KV_HEADS, VT_ROWS, rows), BF16)],
        compiler_params=_params("parallel"),
        name="proj",
    )(h, *consts, tabs["cos_a"], tabs["sin_a"], tabs["cos_b"], tabs["sin_b"])


def _lane_ds(start, size):
    return pl.ds(start if isinstance(start, int) else pl.multiple_of(start, LANE), size)


def _flash_update(s, vt, carry):
    m, acc = carry
    m_new = jnp.maximum(m, jnp.max(s, axis=0, keepdims=True))
    p = jnp.exp2(s - m_new).astype(BF16)
    acc = jnp.exp2(m - m_new) * acc + _dot(vt, p)
    return m_new, acc


def _flash(qs, k_at, vt_at, s_refs, n_real, tk):
    n = len(qs)
    n_blk = n_real // tk
    assert n_real % tk == 0 and n_blk % 2 == 0 and n_blk >= 4

    def scores(i, start, size):
        return lax.dot_general(k_at(i, start, size), qs[i], (((1,), (1,)), ((), ())),
                               preferred_element_type=F32)

    def stage(blk_next, s_next, blk_cur, s_cur, carries):
        out = []
        for i in range(n):
            s_next[i] = scores(i, blk_next * tk, tk)
            out.append(_flash_update(s_cur[i], vt_at(i, blk_cur * tk, tk), carries[i]))
        return tuple(out)

    init = tuple((jnp.full((1, q.shape[0]), NEG_BIG, F32), jnp.zeros((VT_ROWS, q.shape[0]), F32)) for q in qs)
    for i in range(n):
        s_refs[0][i] = scores(i, 0, tk)

    def body(j, carries):
        b0 = pl.multiple_of(2 * j, 2)
        carries = stage(b0 + 1, s_refs[1], b0, s_refs[0], carries)
        return stage(b0 + 2, s_refs[0], b0 + 1, s_refs[1], carries)

    carries = lax.fori_loop(0, n_blk // 2 - 1, body, init)
    carries = stage(n_blk - 1, s_refs[1], n_blk - 2, s_refs[0], carries)
    outs = []
    for i in range(n):
        s_meta = scores(i, n_real, META_PAD)
        is_key = lax.broadcasted_iota(jnp.int32, s_meta.shape, 0) < N_META
        s_meta = jnp.where(is_key, s_meta, NEG_BIG)
        c = _flash_update(s_refs[1][i], vt_at(i, n_real - tk, tk), carries[i])
        _, acc = _flash_update(s_meta, vt_at(i, n_real, META_PAD), c)
        outs.append(acc[:HEAD_V] / acc[HEAD_V:HEAD_V + 1])
    return outs


def _mla_kernel(q_ref, k_ref, vt_ref, o_ref, s0_ref, s1_ref, *, n_real, tk):
    qs = [q_ref[0, 0], q_ref[1, 0]]
    k_at = lambda i, s, n: k_ref[i, 0, _lane_ds(s, n), :]
    vt_at = lambda i, s, n: vt_ref[i, :, _lane_ds(s, n)]
    o0, o1 = _flash(qs, k_at, vt_at, (s0_ref, s1_ref), n_real, tk)
    o_ref[0] = jnp.concatenate([o0, o1], axis=0).T.astype(BF16)


def _mla_attn(qa, ka, vta, batch, seq_rows):
    n_real = seq_rows - META_PAD
    tq = _div_tile(seq_rows, 1792, LANE)
    tk = KV_BLOCK
    q4 = qa.reshape(MLA_HEADS, batch, seq_rows, LANE)
    k4 = ka.reshape(MLA_HEADS, batch, seq_rows, LANE)
    return pl.pallas_call(
        functools.partial(_mla_kernel, n_real=n_real, tk=tk),
        grid=(batch, MLA_HEADS // 2, seq_rows // tq),
        in_specs=[
            pl.BlockSpec((2, 1, tq, LANE), lambda b, j, i: (j, b, i, 0)),
            pl.BlockSpec((2, 1, seq_rows, LANE), lambda b, j, i: (j, b, 0, 0)),
            pl.BlockSpec((2, VT_ROWS, seq_rows), lambda b, j, i: (j, 0, b)),
        ],
        out_specs=pl.BlockSpec((1, tq, LANE), lambda b, j, i: (b, i, j)),
        out_shape=jax.ShapeDtypeStruct((batch, seq_rows, MLA_HEADS * MLA_V), BF16),
        scratch_shapes=[pltpu.VMEM((2, tk, tq), F32), pltpu.VMEM((2, tk, tq), F32)],
        compiler_params=_params("parallel", "parallel", "parallel"),
        name="mla_attn",
    )(q4, k4, vta).reshape(batch * seq_rows, MLA_HEADS * MLA_V)


def _gqa_kernel(q_ref, k_ref, vt_ref, o_ref, s0_ref, s1_ref, *, n_real, tk):
    g, _, tq, _ = q_ref.shape
    q = q_ref[:, 0].reshape(g * tq, LANE)
    k_at = lambda i, s, n: k_ref[0, _lane_ds(s, n), :]
    vt_at = lambda i, s, n: vt_ref[0, :, _lane_ds(s, n)]
    (o,) = _flash([q], k_at, vt_at, (s0_ref, s1_ref), n_real, tk)
    heads = [o[:, i * tq:(i + 1) * tq] for i in range(g)]
    o_ref[0] = jnp.concatenate(heads, axis=0).T.astype(BF16)


def _gqa_attn(qb, kb, vtb, batch, seq_rows):
    n_real = seq_rows - META_PAD
    tq = _div_tile(seq_rows, 640, LANE)
    tk = KV_BLOCK
    q4 = qb.reshape(GQA_HEADS, batch, seq_rows, LANE)
    k3 = kb.reshape(batch, seq_rows, LANE)
    width = GQA_GROUP * GQA_HD
    return pl.pallas_call(
        functools.partial(_gqa_kernel, n_real=n_real, tk=tk),
        grid=(batch, GQA_KV_HEADS, seq_rows // tq),
        in_specs=[
            pl.BlockSpec((GQA_GROUP, 1, tq, LANE), lambda b, j, i: (j, b, i, 0)),
            pl.BlockSpec((1, seq_rows, LANE), lambda b, j, i: (b, 0, 0)),
            pl.BlockSpec((1, VT_ROWS, seq_rows), lambda b, j, i: (j, 0, b)),
        ],
        out_specs=pl.BlockSpec((1, tq, width), lambda b, j, i: (b, i, j)),
        out_shape=jax.ShapeDtypeStruct((batch, seq_rows, GQA_HEADS * GQA_HD), BF16),
        scratch_shapes=[pltpu.VMEM((1, tk, GQA_GROUP * tq), F32), pltpu.VMEM((1, tk, GQA_GROUP * tq), F32)],
        compiler_params=_params("parallel", "parallel", "parallel"),
        name="gqa_attn",
    )(q4, k3, vtb).reshape(batch * seq_rows, GQA_HEADS * GQA_HD)


def _merge_kernel(h_ref, oa_ref, ob_ref, gmix_ref, wg_ref, bg_ref, wao_ref, wbo_ref, wout_ref, o_ref):
    h = h_ref[...]
    d = h.shape[-1]
    hn = _rms(h, gmix_ref[...]).astype(BF16)
    gates = jax.nn.sigmoid(_dot(hn, wg_ref[...]) + bg_ref[...])
    merged = gates[:, :d] * _dot(oa_ref[...], wao_ref[...]) + gates[:, d:] * _dot(ob_ref[...], wbo_ref[...])
    o_ref[...] = h + _dot(merged.astype(BF16), wout_ref[...])


def _merge(h, oa, ob, lw):
    rows, d = h.shape
    tm = _div_tile(rows, 640, LANE)
    row_spec = lambda w: pl.BlockSpec((tm, w), lambda i: (i, 0))
    consts = [lw["g_mix"], lw["w_gates"], lw["b_gate"], lw["w_mla_o"], lw["w_gqa_o"], lw["w_out"]]
    return pl.pallas_call(
        _merge_kernel,
        grid=(rows // tm,),
        in_specs=[row_spec(d), row_spec(oa.shape[1]), row_spec(ob.shape[1])]
        + [_const_spec(c.shape) for c in consts],
        out_specs=row_spec(d),
        out_shape=jax.ShapeDtypeStruct((rows, d), F32),
        compiler_params=_params("parallel"),
        name="merge",
    )(h, oa, ob, *consts)


def _block_diag_mean(groups):
    assert sum(groups) == MXU_DIM
    m = np.zeros((MXU_DIM, MXU_DIM), np.float32)
    c0 = 0
    for gsz in groups:
        m[c0:c0 + gsz, c0:c0 + gsz] = 1.0 / gsz
        c0 += gsz
    return jnp.asarray(m, BF16)


def _rope_angles(pos, dim):
    inv = 1.0 / (ROPE_THETA ** (jnp.arange(0, dim, 2, dtype=jnp.float32) / dim))
    return pos.astype(jnp.float32)[:, None] * inv[None, :]


def _tables(n_real):
    p = jnp.arange(n_real + META_PAD, dtype=jnp.int32)
    real = p < n_real
    meta = (p >= n_real) & (p < n_real + N_META)
    pos_1d = jnp.where(real, p + N_META, jnp.where(meta, p - n_real, 0))
    row = jnp.where(real, p // GRID_W, 0)
    col = jnp.where(real, p % GRID_W, 0)
    a1, ar, ac = _rope_angles(pos_1d, MLA_ROPE), _rope_angles(row, AXIAL_HALF), _rope_angles(col, AXIAL_HALF)
    ones = jnp.ones((p.shape[0], MLA_NOPE), F32)
    pad1 = jnp.ones((p.shape[0], LANE - MLA_QK), F32)
    cos_a = jnp.concatenate([ones, jnp.cos(a1), jnp.cos(a1), pad1], axis=1)
    sin_a = jnp.concatenate([0 * ones, -jnp.sin(a1), jnp.sin(a1), 0 * pad1], axis=1)
    cos_b = jnp.tile(jnp.concatenate([jnp.cos(ar), jnp.cos(ar), jnp.cos(ac), jnp.cos(ac)], axis=1), (1, 2))
    sin_b = jnp.tile(jnp.concatenate([-jnp.sin(ar), jnp.sin(ar), -jnp.sin(ac), jnp.sin(ac)], axis=1), (1, 2))
    return dict(
        cos_a=cos_a, sin_a=sin_a, cos_b=cos_b, sin_b=sin_b,
        bd_a=_block_diag_mean([MLA_NOPE, MLA_ROPE, LANE - MLA_QK] * 2),
        bd_c=_block_diag_mean([GQA_HD] * 4),
        bd_m=_block_diag_mean([MLA_NOPE, MLA_ROPE, LANE - MLA_QK, GQA_HD, GQA_HD]),
    )


def _prep_layer(ffn1_norm, ffn1_w13, ffn1_w2, mix_norm, w_in, b_gate, q_a_norm, kv_a_norm, w_uq, w_ukv,
                mla_q_norm, mla_k_norm, gqa_q_norm, gqa_k_norm, w_mla_o, w_gqa_o, w_out,
                ffn2_norm, ffn2_w13, ffn2_w2, chunks):
    d = w_in.shape[0]
    d_ff = ffn1_w2.shape[0]

    def w13r(w13):
        pieces = []
        for c0, w in chunks:
            pieces += [w13[:, c0:c0 + w], w13[:, d_ff + c0:d_ff + c0 + w]]
        return jnp.concatenate(pieces, axis=1).astype(BF16)

    cuts = np.cumsum([0, Q_LORA, KV_LORA, MLA_ROPE, GQA_HEADS * GQA_HD, GQA_KV_HEADS * GQA_HD,
                      GQA_KV_HEADS * GQA_HD]).tolist()
    w_cq, w_ckv, w_kr, w_qg, w_kg, w_vg = (w_in[:, cuts[i]:cuts[i + 1]] for i in range(6))
    w_gates = w_in[:, cuts[6]:]
    z64 = jnp.zeros((d, GQA_HD), F32)
    qg_cols = []
    for hd in range(GQA_HEADS):
        wq = w_qg[:, hd * GQA_HD:(hd + 1) * GQA_HD]
        qg_cols += [wq, z64] if hd // GQA_GROUP == 0 else [z64, wq]
    kr_cols = [jnp.zeros((d, MLA_NOPE), F32), w_kr, jnp.zeros((d, LANE - MLA_QK), F32)]
    w_in_s = jnp.concatenate(qg_cols + kr_cols + [w_kg, w_vg, w_cq, w_ckv], axis=1).astype(BF16)
    assert w_in_s.shape[1] == _C_END

    uq_cols, ukv_cols, v_cols = [], [], []
    for hd in range(MLA_HEADS):
        uq_cols += [w_uq[:, hd * MLA_QK:(hd + 1) * MLA_QK], jnp.zeros((Q_LORA, LANE - MLA_QK), F32)]
        base = hd * (MLA_NOPE + MLA_V)
        ukv_cols += [w_ukv[:, base:base + MLA_NOPE], jnp.zeros((KV_LORA, LANE - MLA_NOPE), F32)]
        v_cols += [w_ukv[:, base + MLA_NOPE:base + MLA_NOPE + MLA_V]]
    w_uq_p = jnp.concatenate(uq_cols, axis=1).astype(BF16)
    w_ukv_p = jnp.concatenate(ukv_cols + v_cols, axis=1).astype(BF16)

    zq = jnp.zeros((LANE - MLA_QK,), F32)
    ga_q = jnp.tile(jnp.concatenate([mla_q_norm, zq]), MLA_HEADS) * (MLA_QK ** -0.5 * LOG2E)
    ga_k = jnp.tile(jnp.concatenate([mla_k_norm[:MLA_NOPE], jnp.zeros((LANE - MLA_NOPE,), F32)]), MLA_HEADS)
    ga_kr = jnp.concatenate([jnp.zeros((MLA_NOPE,), F32), mla_k_norm[MLA_NOPE:], zq])
    z1 = jnp.zeros((GQA_HD,), F32)
    gb_q = jnp.concatenate([jnp.concatenate([gqa_q_norm, z1] if hd // GQA_GROUP == 0 else [z1, gqa_q_norm])
                            for hd in range(GQA_HEADS)]) * (GQA_HD ** -0.5 * LOG2E)
    gb_k = jnp.tile(gqa_k_norm, GQA_KV_HEADS)
    row = lambda v: v.reshape(1, -1).astype(F32)
    return dict(
        g_ffn1=row(ffn1_norm), w13r_1=w13r(ffn1_w13), w2_1=ffn1_w2.astype(BF16),
        g_ffn2=row(ffn2_norm), w13r_2=w13r(ffn2_w13), w2_2=ffn2_w2.astype(BF16),
        g_mix=row(mix_norm), w_in_s=w_in_s, g_q=row(q_a_norm), g_kv=row(kv_a_norm),
        w_uq_p=w_uq_p, w_ukv_p=w_ukv_p,
        ga_q=row(ga_q), ga_k=row(ga_k), ga_kr=row(ga_kr), gb_q=row(gb_q), gb_k=row(gb_k),
        w_gates=w_gates.astype(BF16), b_gate=row(b_gate),
        w_mla_o=w_mla_o.astype(BF16), w_gqa_o=w_gqa_o.astype(BF16), w_out=w_out.astype(BF16),
    )


def _trunk(x, meta_tokens, layers, chunks):
    batch, n_real, d = x.shape
    assert n_real % GRID_W == 0 and n_real % (4 * KV_BLOCK) == 0
    seq_rows = n_real + META_PAD
    tail = jnp.concatenate([meta_tokens.astype(x.dtype), jnp.zeros((META_PAD - N_META, d), x.dtype)], axis=0)
    h = jnp.concatenate([x, jnp.broadcast_to(tail[None], (batch, META_PAD, d))], axis=1)
    h = h.reshape(batch * seq_rows, d)
    tabs = _tables(n_real)
    for lw in layers:
        h = _ffn(h, lw["g_ffn1"], lw["w13r_1"], lw["w2_1"], chunks)
        qa, ka, vta, qb, kb, vtb = _proj(h, lw, tabs, seq_rows)
        oa = _mla_attn(qa, ka, vta, batch, seq_rows)
        ob = _gqa_attn(qb, kb, vtb, batch, seq_rows)
        h = _merge(h, oa, ob, lw)
        h = _ffn(h, lw["g_ffn2"], lw["w13r_2"], lw["w2_2"], chunks)
    return h.reshape(batch, seq_rows, d)[:, :n_real]


def kernel(x_prompt, x_sample, meta_tokens, ffn1_norm, ffn1_w13, ffn1_w2, mix_norm, w_in, b_gate, q_a_norm,
           kv_a_norm, w_uq, w_ukv, mla_q_norm, mla_k_norm, gqa_q_norm, gqa_k_norm, w_mla_o, w_gqa_o, w_out,
           ffn2_norm, ffn2_w13, ffn2_w2):
    per_layer = (ffn1_norm, ffn1_w13, ffn1_w2, mix_norm, w_in, b_gate, q_a_norm, kv_a_norm, w_uq, w_ukv,
                 mla_q_norm, mla_k_norm, gqa_q_norm, gqa_k_norm, w_mla_o, w_gqa_o, w_out,
                 ffn2_norm, ffn2_w13, ffn2_w2)
    chunks = _ff_chunks(ffn1_w2.shape[1])
    layers = [_prep_layer(*(p[l] for p in per_layer), chunks=chunks) for l in range(ffn1_norm.shape[0])]
    return (_trunk(x_prompt, meta_tokens, layers, chunks), _trunk(x_sample, meta_tokens, layers, chunks))
```

```python
import functools
import math

import numpy as np
import jax
import jax.numpy as jnp
from jax import lax
from jax.experimental import pallas as pl
from jax.experimental.pallas import tpu as pltpu

N_META = 16
GRID_W = 64
EPS = 1e-6
ROPE_THETA = 10000.0
MLA_HEADS = 8
MLA_NOPE = 64
MLA_ROPE = 32
MLA_QK = MLA_NOPE + MLA_ROPE
MLA_V = 64
Q_LORA = 384
KV_LORA = 256
GQA_HEADS = 8
GQA_KV_HEADS = 2
GQA_GROUP = GQA_HEADS // GQA_KV_HEADS
GQA_HD = 64
AXIAL_HALF = GQA_HD // 2
ROPE_PAIR = 16
HEAD_V = 64

LANE = 128
MXU_DIM = 256
BF16_ROWS = 16
VMEM_LIMIT = 56 * 1024 * 1024

META_PAD = LANE
VT_ROWS = HEAD_V + BF16_ROWS
KV_BLOCK = 512
GQA_STREAMS = 2
LOG2E = math.log2(math.e)
NEG_BIG = -1e30

BF16 = jnp.bfloat16
F32 = jnp.float32


def _div_tile(n, target, mult):
    best = None
    for t in range(mult, min(n, target) + 1, mult):
        if n % t == 0:
            best = t
    assert best is not None, (n, target, mult)
    return best


def _ff_chunks(d_ff):
    chunks, c0 = [], 0
    while c0 < d_ff:
        w = min(3 * MXU_DIM, d_ff - c0)
        chunks.append((c0, w))
        c0 += w
    return tuple(chunks)


def _dot(a, b):
    return jnp.dot(a, b, preferred_element_type=F32)


def _rms(x, g):
    return x * lax.rsqrt(jnp.mean(x * x, axis=-1, keepdims=True) + EPS) * g


def _const_spec(shape):
    nd = len(shape)
    return pl.BlockSpec(shape, lambda *_: (0,) * nd, pipeline_mode=pl.Buffered(1))


def _params(*sem):
    return pltpu.CompilerParams(dimension_semantics=sem, vmem_limit_bytes=VMEM_LIMIT)


def _ffn_kernel(h_ref, g_ref, w13_ref, w2_ref, o_ref, *, chunks):
    h = h_ref[...]
    xn = _rms(h, g_ref[...]).astype(BF16)
    acc = None
    off = 0
    for c0, w in chunks:
        gu = _dot(xn, w13_ref[:, off:off + 2 * w])
        gate, up = gu[:, :w], gu[:, w:]
        a = (gate * jax.nn.sigmoid(gate) * up).astype(BF16)
        part = _dot(a, w2_ref[c0:c0 + w, :])
        acc = part if acc is None else acc + part
        off += 2 * w
    o_ref[...] = h + 0.5 * acc


def _ffn(h, g, w13r, w2, chunks, real_only=None):
    rows, d = h.shape
    if real_only is None:
        tm = _div_tile(rows, 640, LANE)
        grid, sem = (rows // tm,), ("parallel",)
        spec = pl.BlockSpec((tm, d), lambda i: (i, 0))
        out_shape = jax.ShapeDtypeStruct((rows, d), F32)
    else:
        batch, seq_rows, n_real = real_only
        tm = _div_tile(n_real, 640, LANE)
        h = h.reshape(batch, seq_rows, d)
        grid, sem = (batch, n_real // tm), ("parallel", "parallel")
        spec = pl.BlockSpec((None, tm, d), lambda b, i: (b, i, 0))
        out_shape = jax.ShapeDtypeStruct((batch, n_real, d), F32)
    return pl.pallas_call(
        functools.partial(_ffn_kernel, chunks=chunks),
        grid=grid,
        in_specs=[spec, _const_spec(g.shape), _const_spec(w13r.shape), _const_spec(w2.shape)],
        out_specs=spec,
        out_shape=out_shape,
        compiler_params=_params(*sem),
        name="ffn",
    )(h, g, w13r, w2)


_C_QG = 0
_C_KR = 512
_C_KG = 640
_C_VG = 768
_C_CQ = 896
_C_CKV = 1280
_C_END = 1536


def _group_ms(x, bd):
    return _dot((x * x).astype(BF16), bd)


def _rope(y, cos, sin, first_half):
    partner = jnp.where(first_half, pltpu.roll(y, LANE - ROPE_PAIR, 1), pltpu.roll(y, ROPE_PAIR, 1))
    return y * cos + partner * sin


def _store_vt(vt_ref, head0, v_pair, ones_tile):
    vt = v_pair.T
    for k in range(2):
        vt_ref[head0 + k, :HEAD_V, :] = vt[k * HEAD_V:(k + 1) * HEAD_V].astype(BF16)
        vt_ref[head0 + k, HEAD_V:, :] = ones_tile


def _proj_kernel(h_ref, gmix_ref, win_ref, gq_ref, gkv_ref, wuq_ref, wukv_ref,
                 bd_a_ref, bd_c_ref, bd_m_ref, gaq_ref, gak_ref, gakr_ref, gbq_ref, gbk_ref,
                 cos_a_ref, sin_a_ref, cos_b_ref, sin_b_ref,
                 qa_ref, ka_ref, vta_ref, qb_ref, kb_ref, vtb_ref):
    tm = h_ref.shape[0]
    hn = _rms(h_ref[...], gmix_ref[...]).astype(BF16)
    z = _dot(hn, win_ref[...])
    cq = _rms(z[:, _C_CQ:_C_CKV], gq_ref[...]).astype(BF16)
    ckv = _rms(z[:, _C_CKV:_C_END], gkv_ref[...]).astype(BF16)
    qa = _dot(cq, wuq_ref[...])
    kva = _dot(ckv, wukv_ref[...])
    n_kn = MLA_HEADS * MLA_NOPE

    lane = lax.broadcasted_iota(jnp.int32, (tm, LANE), 1)
    first_half = (lane % (2 * ROPE_PAIR)) < ROPE_PAIR
    low_half = lane < HEAD_V
    cos_a, sin_a = cos_a_ref[...], sin_a_ref[...]
    cos_b, sin_b = cos_b_ref[...], sin_b_ref[...]
    bd_a, bd_c, bd_m = bd_a_ref[...], bd_c_ref[...], bd_m_ref[...]
    ones_tile = (lax.broadcasted_iota(jnp.int32, (BF16_ROWS, tm), 0) == 0).astype(F32).astype(BF16)

    x = z[:, _C_KR:_C_VG]
    y = x * lax.rsqrt(_group_ms(x, bd_m) + EPS)
    kr = _rope(y[:, :LANE] * gakr_ref[...], cos_a, sin_a, first_half)
    kb_ref[...] = _rope(y[:, LANE:] * gbk_ref[...], cos_b, sin_b, first_half).astype(BF16)
    _store_vt(vtb_ref, 0, z[:, _C_VG:_C_CQ], ones_tile)
    for c in range(MLA_HEADS // 2):
        v0 = n_kn + c * LANE
        _store_vt(vta_ref, 2 * c, kva[:, v0:v0 + LANE], ones_tile)

    for c in range(MLA_HEADS * LANE // MXU_DIM):
        sl = slice(c * MXU_DIM, (c + 1) * MXU_DIM)
        x = qa[:, sl]
        y = x * lax.rsqrt(_group_ms(x, bd_a) + EPS) * gaq_ref[:, sl]
        for half in range(2):
            t = _rope(y[:, half * LANE:(half + 1) * LANE], cos_a, sin_a, first_half)
            qa_ref[2 * c + half] = t.astype(BF16)

    for c in range(n_kn // MXU_DIM):
        sl = slice(c * MXU_DIM, (c + 1) * MXU_DIM)
        xk = kva[:, sl]
        yk = xk * lax.rsqrt(_group_ms(xk, bd_c) + EPS) * gak_ref[:, sl]
        for half in range(2):
            t = yk[:, half * LANE:(half + 1) * LANE]
            for k, tile in enumerate((t, pltpu.roll(t, HEAD_V, 1))):
                ka_ref[4 * c + 2 * half + k] = (jnp.where(low_half, tile, 0.0) + kr).astype(BF16)

    for kv in range(GQA_KV_HEADS):
        sl = slice(kv * MXU_DIM, (kv + 1) * MXU_DIM)
        xg = z[:, sl]
        yg = xg * lax.rsqrt(_group_ms(xg, bd_c) + EPS) * gbq_ref[:, sl]
        keep = low_half if kv == 0 else jnp.logical_not(low_half)
        for half in range(2):
            t = _rope(yg[:, half * LANE:(half + 1) * LANE], cos_b, sin_b, first_half)
            pair = (t, pltpu.roll(t, GQA_HD, 1)) if kv == 0 else (pltpu.roll(t, GQA_HD, 1), t)
            for k in range(2):
                head = GQA_GROUP * kv + 2 * half + k
                qb_ref[head] = jnp.where(keep, pair[k], 0.0).astype(BF16)


def _proj(h, lw, tabs, seq_rows):
    rows, d = h.shape
    tm = _div_tile(seq_rows, 640, LANE)
    per_seq = seq_rows // tm
    row_spec = lambda w: pl.BlockSpec((tm, w), lambda i: (i, 0))
    tab_spec = pl.BlockSpec((tm, LANE), lambda i: (i % per_seq, 0))
    head_spec = pl.BlockSpec((MLA_HEADS, tm, LANE), lambda i: (0, i, 0))
    vt_spec = lambda n: pl.BlockSpec((n, VT_ROWS, tm), lambda i: (0, 0, i))
    consts = [lw["g_mix"], lw["w_in_s"], lw["g_q"], lw["g_kv"], lw["w_uq_p"], lw["w_ukv_p"],
              tabs["bd_a"], tabs["bd_c"], tabs["bd_m"],
              lw["ga_q"], lw["ga_k"], lw["ga_kr"], lw["gb_q"], lw["gb_k"]]
    heads = jax.ShapeDtypeStruct((MLA_HEADS, rows, LANE), BF16)
    return pl.pallas_call(
        _proj_kernel,
        grid=(rows // tm,),
        in_specs=[row_spec(d)] + [_const_spec(c.shape) for c in consts] + [tab_spec] * 4,
        out_specs=[head_spec, head_spec, vt_spec(MLA_HEADS), head_spec, row_spec(LANE),
                   vt_spec(GQA_KV_HEADS)],
        out_shape=[heads, heads, jax.ShapeDtypeStruct((MLA_HEADS, VT_ROWS, rows), BF16), heads,
                   jax.ShapeDtypeStruct((rows, LANE), BF16),
                   jax.ShapeDtypeStruct((GQA_KV_HEADS, VT_ROWS, rows), BF16)],
        compiler_params=_params("parallel"),
        name="proj",
    )(h, *consts, tabs["cos_a"], tabs["sin_a"], tabs["cos_b"], tabs["sin_b"])


def _lane_ds(start, size):
    return pl.ds(start if isinstance(start, int) else pl.multiple_of(start, LANE), size)


def _flash_update(s, vt, carry):
    m, acc = carry
    m_new = jnp.maximum(m, jnp.max(s, axis=0, keepdims=True))
    p = jnp.exp2(s - m_new).astype(BF16)
    acc = jnp.exp2(m - m_new) * acc + _dot(vt, p)
    return m_new, acc


def _flash(qs, k_at, vt_at, s_refs, n_real, tk):
    n = len(qs)
    n_blk = n_real // tk
    assert n_real % tk == 0 and n_blk % 2 == 0 and n_blk >= 4

    def scores(i, start, size):
        return lax.dot_general(k_at(i, start, size), qs[i], (((1,), (1,)), ((), ())),
                               preferred_element_type=F32)

    def stage(blk_next, s_next, blk_cur, s_cur, carries):
        out = []
        for i in range(n):
            s_next[i] = scores(i, blk_next * tk, tk)
            out.append(_flash_update(s_cur[i], vt_at(i, blk_cur * tk, tk), carries[i]))
        return tuple(out)

    init = tuple((jnp.full((1, q.shape[0]), NEG_BIG, F32), jnp.zeros((VT_ROWS, q.shape[0]), F32)) for q in qs)
    for i in range(n):
        s_refs[0][i] = scores(i, 0, tk)

    def body(j, carries):
        b0 = pl.multiple_of(2 * j, 2)
        carries = stage(b0 + 1, s_refs[1], b0, s_refs[0], carries)
        return stage(b0 + 2, s_refs[0], b0 + 1, s_refs[1], carries)

    carries = lax.fori_loop(0, n_blk // 2 - 1, body, init)
    carries = stage(n_blk - 1, s_refs[1], n_blk - 2, s_refs[0], carries)
    outs = []
    for i in range(n):
        s_meta = scores(i, n_real, META_PAD)
        is_key = lax.broadcasted_iota(jnp.int32, s_meta.shape, 0) < N_META
        s_meta = jnp.where(is_key, s_meta, NEG_BIG)
        c = _flash_update(s_refs[1][i], vt_at(i, n_real - tk, tk), carries[i])
        _, acc = _flash_update(s_meta, vt_at(i, n_real, META_PAD), c)
        outs.append(acc[:HEAD_V] / acc[HEAD_V:HEAD_V + 1])
    return outs


def _mla_kernel(q_ref, k_ref, vt_ref, o_ref, s0_ref, s1_ref, *, n_real, tk):
    qs = [q_ref[0, 0], q_ref[1, 0]]
    k_at = lambda i, s, n: k_ref[i, 0, _lane_ds(s, n), :]
    vt_at = lambda i, s, n: vt_ref[i, :, _lane_ds(s, n)]
    o0, o1 = _flash(qs, k_at, vt_at, (s0_ref, s1_ref), n_real, tk)
    o_ref[0] = jnp.concatenate([o0, o1], axis=0).T.astype(BF16)


def _kv_tile(n_real):
    return _div_tile(n_real // 4, KV_BLOCK, LANE)


def _mla_attn(qa, ka, vta, batch, seq_rows):
    n_real = seq_rows - META_PAD
    tq = _div_tile(seq_rows, 1792, LANE)
    tk = _kv_tile(n_real)
    q4 = qa.reshape(MLA_HEADS, batch, seq_rows, LANE)
    k4 = ka.reshape(MLA_HEADS, batch, seq_rows, LANE)
    return pl.pallas_call(
        functools.partial(_mla_kernel, n_real=n_real, tk=tk),
        grid=(batch, MLA_HEADS // 2, seq_rows // tq),
        in_specs=[
            pl.BlockSpec((2, 1, tq, LANE), lambda b, j, i: (j, b, i, 0)),
            pl.BlockSpec((2, 1, seq_rows, LANE), lambda b, j, i: (j, b, 0, 0)),
            pl.BlockSpec((2, VT_ROWS, seq_rows), lambda b, j, i: (j, 0, b)),
        ],
        out_specs=pl.BlockSpec((1, tq, LANE), lambda b, j, i: (b, i, j)),
        out_shape=jax.ShapeDtypeStruct((batch, seq_rows, MLA_HEADS * MLA_V), BF16),
        scratch_shapes=[pltpu.VMEM((2, tk, tq), F32), pltpu.VMEM((2, tk, tq), F32)],
        compiler_params=_params("parallel", "parallel", "parallel"),
        name="mla_attn",
    )(q4, k4, vta).reshape(batch * seq_rows, MLA_HEADS * MLA_V)


def _gqa_kernel(q_ref, k_ref, vt_ref, o_ref, s0_ref, s1_ref, *, n_real, tk):
    g, _, tq, _ = q_ref.shape
    per = g // GQA_STREAMS
    qs = [q_ref[i * per:(i + 1) * per, 0].reshape(per * tq, LANE) for i in range(GQA_STREAMS)]
    k_at = lambda i, s, n: k_ref[0, _lane_ds(s, n), :]
    vt_at = lambda i, s, n: vt_ref[0, :, _lane_ds(s, n)]
    outs = _flash(qs, k_at, vt_at, (s0_ref, s1_ref), n_real, tk)
    heads = [o[:, i * tq:(i + 1) * tq] for o in outs for i in range(per)]
    o_ref[0] = jnp.concatenate(heads, axis=0).T.astype(BF16)


def _gqa_attn(qb, kb, vtb, batch, seq_rows):
    n_real = seq_rows - META_PAD
    per = GQA_GROUP // GQA_STREAMS
    tq = _div_tile(seq_rows, 704, LANE // per)
    tk = _kv_tile(n_real)
    q4 = qb.reshape(GQA_HEADS, batch, seq_rows, LANE)
    k3 = kb.reshape(batch, seq_rows, LANE)
    width = GQA_GROUP * GQA_HD
    s_buf = pltpu.VMEM((GQA_STREAMS, tk, per * tq), F32)
    return pl.pallas_call(
        functools.partial(_gqa_kernel, n_real=n_real, tk=tk),
        grid=(batch, GQA_KV_HEADS, seq_rows // tq),
        in_specs=[
            pl.BlockSpec((GQA_GROUP, 1, tq, LANE), lambda b, j, i: (j, b, i, 0)),
            pl.BlockSpec((1, seq_rows, LANE), lambda b, j, i: (b, 0, 0)),
            pl.BlockSpec((1, VT_ROWS, seq_rows), lambda b, j, i: (j, 0, b)),
        ],
        out_specs=pl.BlockSpec((1, tq, width), lambda b, j, i: (b, i, j)),
        out_shape=jax.ShapeDtypeStruct((batch, seq_rows, GQA_HEADS * GQA_HD), BF16),
        scratch_shapes=[s_buf, s_buf],
        compiler_params=_params("parallel", "parallel", "parallel"),
        name="gqa_attn",
    )(q4, k3, vtb).reshape(batch * seq_rows, GQA_HEADS * GQA_HD)


def _merge_kernel(h_ref, oa_ref, ob_ref, gmix_ref, wg_ref, bg_ref, wao_ref, wbo_ref, wout_ref, o_ref):
    h = h_ref[...]
    d = h.shape[-1]
    hn = _rms(h, gmix_ref[...]).astype(BF16)
    gates = jax.nn.sigmoid(_dot(hn, wg_ref[...]) + bg_ref[...])
    merged = gates[:, :d] * _dot(oa_ref[...], wao_ref[...]) + gates[:, d:] * _dot(ob_ref[...], wbo_ref[...])
    o_ref[...] = h + _dot(merged.astype(BF16), wout_ref[...])


def _merge(h, oa, ob, lw):
    rows, d = h.shape
    tm = _div_tile(rows, 640, LANE)
    row_spec = lambda w: pl.BlockSpec((tm, w), lambda i: (i, 0))
    consts = [lw["g_mix"], lw["w_gates"], lw["b_gate"], lw["w_mla_o"], lw["w_gqa_o"], lw["w_out"]]
    return pl.pallas_call(
        _merge_kernel,
        grid=(rows // tm,),
        in_specs=[row_spec(d), row_spec(oa.shape[1]), row_spec(ob.shape[1])]
        + [_const_spec(c.shape) for c in consts],
        out_specs=row_spec(d),
        out_shape=jax.ShapeDtypeStruct((rows, d), F32),
        compiler_params=_params("parallel"),
        name="merge",
    )(h, oa, ob, *consts)


def _block_diag_mean(groups):
    assert sum(groups) == MXU_DIM
    m = np.zeros((MXU_DIM, MXU_DIM), np.float32)
    c0 = 0
    for gsz in groups:
        m[c0:c0 + gsz, c0:c0 + gsz] = 1.0 / gsz
        c0 += gsz
    return jnp.asarray(m, BF16)


def _rope_angles(pos, dim):
    inv = 1.0 / (ROPE_THETA ** (jnp.arange(0, dim, 2, dtype=jnp.float32) / dim))
    return pos.astype(jnp.float32)[:, None] * inv[None, :]


def _tables(n_real):
    p = jnp.arange(n_real + META_PAD, dtype=jnp.int32)
    real = p < n_real
    meta = (p >= n_real) & (p < n_real + N_META)
    pos_1d = jnp.where(real, p + N_META, jnp.where(meta, p - n_real, 0))
    row = jnp.where(real, p // GRID_W, 0)
    col = jnp.where(real, p % GRID_W, 0)
    a1, ar, ac = _rope_angles(pos_1d, MLA_ROPE), _rope_angles(row, AXIAL_HALF), _rope_angles(col, AXIAL_HALF)
    ones = jnp.ones((p.shape[0], MLA_NOPE), F32)
    pad1 = jnp.ones((p.shape[0], LANE - MLA_QK), F32)
    cos_a = jnp.concatenate([ones, jnp.cos(a1), jnp.cos(a1), pad1], axis=1)
    sin_a = jnp.concatenate([0 * ones, -jnp.sin(a1), jnp.sin(a1), 0 * pad1], axis=1)
    cos_b = jnp.tile(jnp.concatenate([jnp.cos(ar), jnp.cos(ar), jnp.cos(ac), jnp.cos(ac)], axis=1), (1, 2))
    sin_b = jnp.tile(jnp.concatenate([-jnp.sin(ar), jnp.sin(ar), -jnp.sin(ac), jnp.sin(ac)], axis=1), (1, 2))
    return dict(
        cos_a=cos_a, sin_a=sin_a, cos_b=cos_b, sin_b=sin_b,
        bd_a=_block_diag_mean([MLA_NOPE, MLA_ROPE, LANE - MLA_QK] * 2),
        bd_c=_block_diag_mean([GQA_HD] * 4),
        bd_m=_block_diag_mean([MLA_NOPE, MLA_ROPE, LANE - MLA_QK, GQA_HD, GQA_HD]),
    )


def _prep_layer(ffn1_norm, ffn1_w13, ffn1_w2, mix_norm, w_in, b_gate, q_a_norm, kv_a_norm, w_uq, w_ukv,
                mla_q_norm, mla_k_norm, gqa_q_norm, gqa_k_norm, w_mla_o, w_gqa_o, w_out,
                ffn2_norm, ffn2_w13, ffn2_w2, chunks):
    d = w_in.shape[0]
    d_ff = ffn1_w2.shape[0]

    def w13r(w13):
        pieces = []
        for c0, w in chunks:
            pieces += [w13[:, c0:c0 + w], w13[:, d_ff + c0:d_ff + c0 + w]]
        return jnp.concatenate(pieces, axis=1).astype(BF16)

    cuts = np.cumsum([0, Q_LORA, KV_LORA, MLA_ROPE, GQA_HEADS * GQA_HD, GQA_KV_HEADS * GQA_HD,
                      GQA_KV_HEADS * GQA_HD]).tolist()
    w_cq, w_ckv, w_kr, w_qg, w_kg, w_vg = (w_in[:, cuts[i]:cuts[i + 1]] for i in range(6))
    w_gates = w_in[:, cuts[6]:]
    kr_cols = [jnp.zeros((d, MLA_NOPE), F32), w_kr, jnp.zeros((d, LANE - MLA_QK), F32)]
    w_in_s = jnp.concatenate([w_qg] + kr_cols + [w_kg, w_vg, w_cq, w_ckv], axis=1).astype(BF16)
    assert w_in_s.shape[1] == _C_END

    uq_cols, ukv_cols, v_cols = [], [], []
    for hd in range(MLA_HEADS):
        uq_cols += [w_uq[:, hd * MLA_QK:(hd + 1) * MLA_QK], jnp.zeros((Q_LORA, LANE - MLA_QK), F32)]
        base = hd * (MLA_NOPE + MLA_V)
        ukv_cols += [w_ukv[:, base:base + MLA_NOPE]]
        v_cols += [w_ukv[:, base + MLA_NOPE:base + MLA_NOPE + MLA_V]]
    w_uq_p = jnp.concatenate(uq_cols, axis=1).astype(BF16)
    w_ukv_p = jnp.concatenate(ukv_cols + v_cols, axis=1).astype(BF16)

    zq = jnp.zeros((LANE - MLA_QK,), F32)
    ga_q = jnp.tile(jnp.concatenate([mla_q_norm, zq]), MLA_HEADS) * (MLA_QK ** -0.5 * LOG2E)
    ga_k = jnp.tile(mla_k_norm[:MLA_NOPE], MLA_HEADS)
    ga_kr = jnp.concatenate([jnp.zeros((MLA_NOPE,), F32), mla_k_norm[MLA_NOPE:], zq])
    gb_q = jnp.tile(gqa_q_norm, GQA_HEADS) * (GQA_HD ** -0.5 * LOG2E)
    gb_k = jnp.tile(gqa_k_norm, GQA_KV_HEADS)
    row = lambda v: v.reshape(1, -1).astype(F32)
    return dict(
        g_ffn1=row(ffn1_norm), w13r_1=w13r(ffn1_w13), w2_1=ffn1_w2.astype(BF16),
        g_ffn2=row(ffn2_norm), w13r_2=w13r(ffn2_w13), w2_2=ffn2_w2.astype(BF16),
        g_mix=row(mix_norm), w_in_s=w_in_s, g_q=row(q_a_norm), g_kv=row(kv_a_norm),
        w_uq_p=w_uq_p, w_ukv_p=w_ukv_p,
        ga_q=row(ga_q), ga_k=row(ga_k), ga_kr=row(ga_kr), gb_q=row(gb_q), gb_k=row(gb_k),
        w_gates=w_gates.astype(BF16), b_gate=row(b_gate),
        w_mla_o=w_mla_o.astype(BF16), w_gqa_o=w_gqa_o.astype(BF16), w_out=w_out.astype(BF16),
    )


def _trunk(x, meta_tokens, layers, chunks):
    batch, n_real, d = x.shape
    assert n_real % GRID_W == 0 and n_real % (8 * LANE) == 0
    seq_rows = n_real + META_PAD
    tail = jnp.concatenate([meta_tokens.astype(x.dtype), jnp.zeros((META_PAD - N_META, d), x.dtype)], axis=0)
    h = jnp.concatenate([x, jnp.broadcast_to(tail[None], (batch, META_PAD, d))], axis=1)
    h = h.reshape(batch * seq_rows, d)
    tabs = _tables(n_real)
    for idx, lw in enumerate(layers):
        h = _ffn(h, lw["g_ffn1"], lw["w13r_1"], lw["w2_1"], chunks)
        qa, ka, vta, qb, kb, vtb = _proj(h, lw, tabs, seq_rows)
        oa = _mla_attn(qa, ka, vta, batch, seq_rows)
        ob = _gqa_attn(qb, kb, vtb, batch, seq_rows)
        h = _merge(h, oa, ob, lw)
        real_only = (batch, seq_rows, n_real) if idx == len(layers) - 1 else None
        h = _ffn(h, lw["g_ffn2"], lw["w13r_2"], lw["w2_2"], chunks, real_only)
    return h


def kernel(x_prompt, x_sample, meta_tokens, ffn1_norm, ffn1_w13, ffn1_w2, mix_norm, w_in, b_gate, q_a_norm,
           kv_a_norm, w_uq, w_ukv, mla_q_norm, mla_k_norm, gqa_q_norm, gqa_k_norm, w_mla_o, w_gqa_o, w_out,
           ffn2_norm, ffn2_w13, ffn2_w2):
    per_layer = (ffn1_norm, ffn1_w13, ffn1_w2, mix_norm, w_in, b_gate, q_a_norm, kv_a_norm, w_uq, w_ukv,
                 mla_q_norm, mla_k_norm, gqa_q_norm, gqa_k_norm, w_mla_o, w_gqa_o, w_out,
                 ffn2_norm, ffn2_w13, ffn2_w2)
    chunks = _ff_chunks(ffn1_w2.shape[1])
    layers = [_prep_layer(*(p[l] for p in per_layer), chunks=chunks) for l in range(ffn1_norm.shape[0])]
    return (_trunk(x_prompt, meta_tokens, layers, chunks), _trunk(x_sample, meta_tokens, layers, chunks))
```

```python
import functools
import math

import numpy as np
import jax
import jax.numpy as jnp
from jax import lax
from jax.experimental import pallas as pl
from jax.experimental.pallas import tpu as pltpu

N_META = 16
GRID_W = 64
EPS = 1e-6
ROPE_THETA = 10000.0
MLA_HEADS = 8
MLA_NOPE = 64
MLA_ROPE = 32
MLA_QK = MLA_NOPE + MLA_ROPE
MLA_V = 64
Q_LORA = 384
KV_LORA = 256
GQA_HEADS = 8
GQA_KV_HEADS = 2
GQA_GROUP = GQA_HEADS // GQA_KV_HEADS
GQA_HD = 64
AXIAL_HALF = GQA_HD // 2
ROPE_PAIR = 16
HEAD_V = 64

LANE = 128
MXU_DIM = 256
BF16_ROWS = 16
VMEM_LIMIT = 56 * 1024 * 1024

META_PAD = LANE
VT_ROWS = HEAD_V + BF16_ROWS
KV_BLOCK = 512
ATTN_UNROLL = 4
GQA_STREAMS = 2
LOG2E = math.log2(math.e)
NEG_BIG = -1e30

BF16 = jnp.bfloat16
F32 = jnp.float32


def _div_tile(n, target, mult):
    best = None
    for t in range(mult, min(n, target) + 1, mult):
        if n % t == 0:
            best = t
    assert best is not None, (n, target, mult)
    return best


def _ff_chunks(d_ff):
    chunks, c0 = [], 0
    while c0 < d_ff:
        w = min(3 * MXU_DIM, d_ff - c0)
        chunks.append((c0, w))
        c0 += w
    return tuple(chunks)


def _dot(a, b):
    return jnp.dot(a, b, preferred_element_type=F32)


def _rms(x, g):
    return x * lax.rsqrt(jnp.mean(x * x, axis=-1, keepdims=True) + EPS) * g


def _const_spec(shape):
    nd = len(shape)
    return pl.BlockSpec(shape, lambda *_: (0,) * nd, pipeline_mode=pl.Buffered(1))


def _params(*sem):
    return pltpu.CompilerParams(dimension_semantics=sem, vmem_limit_bytes=VMEM_LIMIT)


def _ffn_kernel(h_ref, g_ref, w13_ref, w2_ref, o_ref, *, chunks):
    h = h_ref[...]
    xn = _rms(h, g_ref[...]).astype(BF16)
    acc = None
    off = 0
    for c0, w in chunks:
        gu = _dot(xn, w13_ref[:, off:off + 2 * w])
        gate, up = gu[:, :w], gu[:, w:]
        a = (gate * jax.nn.sigmoid(gate) * up).astype(BF16)
        part = _dot(a, w2_ref[c0:c0 + w, :])
        acc = part if acc is None else acc + part
        off += 2 * w
    o_ref[...] = h + 0.5 * acc


def _ffn(h, g, w13r, w2, chunks, real_only=None):
    rows, d = h.shape
    if real_only is None:
        tm = _div_tile(rows, 640, LANE)
        grid, sem = (rows // tm,), ("parallel",)
        spec = pl.BlockSpec((tm, d), lambda i: (i, 0))
        out_shape = jax.ShapeDtypeStruct((rows, d), F32)
    else:
        batch, seq_rows, n_real = real_only
        tm = _div_tile(n_real, 640, LANE)
        h = h.reshape(batch, seq_rows, d)
        grid, sem = (batch, n_real // tm), ("parallel", "parallel")
        spec = pl.BlockSpec((None, tm, d), lambda b, i: (b, i, 0))
        out_shape = jax.ShapeDtypeStruct((batch, n_real, d), F32)
    return pl.pallas_call(
        functools.partial(_ffn_kernel, chunks=chunks),
        grid=grid,
        in_specs=[spec, _const_spec(g.shape), _const_spec(w13r.shape), _const_spec(w2.shape)],
        out_specs=spec,
        out_shape=out_shape,
        compiler_params=_params(*sem),
        name="ffn",
    )(h, g, w13r, w2)


_C_QG = 0
_C_KR = 512
_C_KG = 640
_C_VG = 768
_C_CQ = 896
_C_CKV = 1280
_C_END = 1536


def _group_ms(x, bd):
    return _dot((x * x).astype(BF16), bd)


def _rope(y, cos, sin, first_half):
    partner = jnp.where(first_half, pltpu.roll(y, LANE - ROPE_PAIR, 1), pltpu.roll(y, ROPE_PAIR, 1))
    return y * cos + partner * sin


def _store_vt(vt_ref, head0, v_pair, ones_tile):
    vt = v_pair.T
    for k in range(2):
        vt_ref[head0 + k, :HEAD_V, :] = vt[k * HEAD_V:(k + 1) * HEAD_V].astype(BF16)
        vt_ref[head0 + k, HEAD_V:, :] = ones_tile


def _proj_kernel(h_ref, gmix_ref, win_ref, gq_ref, gkv_ref, wuq_ref, wukv_ref,
                 bd_a_ref, bd_c_ref, bd_m_ref, gaq_ref, gak_ref, gakr_ref, gbq_ref, gbk_ref,
                 cos_a_ref, sin_a_ref, cos_b_ref, sin_b_ref,
                 qa_ref, ka_ref, vta_ref, qb_ref, kb_ref, vtb_ref):
    tm = h_ref.shape[0]
    hn = _rms(h_ref[...], gmix_ref[...]).astype(BF16)
    z = _dot(hn, win_ref[...])
    cq = _rms(z[:, _C_CQ:_C_CKV], gq_ref[...]).astype(BF16)
    ckv = _rms(z[:, _C_CKV:_C_END], gkv_ref[...]).astype(BF16)
    qa = _dot(cq, wuq_ref[...])
    kva = _dot(ckv, wukv_ref[...])
    n_kn = MLA_HEADS * MLA_NOPE

    lane = lax.broadcasted_iota(jnp.int32, (tm, LANE), 1)
    first_half = (lane % (2 * ROPE_PAIR)) < ROPE_PAIR
    low_half = lane < HEAD_V
    cos_a, sin_a = cos_a_ref[...], sin_a_ref[...]
    cos_b, sin_b = cos_b_ref[...], sin_b_ref[...]
    bd_a, bd_c, bd_m = bd_a_ref[...], bd_c_ref[...], bd_m_ref[...]
    ones_tile = (lax.broadcasted_iota(jnp.int32, (BF16_ROWS, tm), 0) == 0).astype(F32).astype(BF16)

    x = z[:, _C_KR:_C_VG]
    y = x * lax.rsqrt(_group_ms(x, bd_m) + EPS)
    kr = _rope(y[:, :LANE] * gakr_ref[...], cos_a, sin_a, first_half)
    kb_ref[...] = _rope(y[:, LANE:] * gbk_ref[...], cos_b, sin_b, first_half).astype(BF16)
    _store_vt(vtb_ref, 0, z[:, _C_VG:_C_CQ], ones_tile)
    for c in range(MLA_HEADS // 2):
        v0 = n_kn + c * LANE
        _store_vt(vta_ref, 2 * c, kva[:, v0:v0 + LANE], ones_tile)

    for c in range(MLA_HEADS * LANE // MXU_DIM):
        sl = slice(c * MXU_DIM, (c + 1) * MXU_DIM)
        x = qa[:, sl]
        y = x * lax.rsqrt(_group_ms(x, bd_a) + EPS) * gaq_ref[:, sl]
        for half in range(2):
            t = _rope(y[:, half * LANE:(half + 1) * LANE], cos_a, sin_a, first_half)
            qa_ref[2 * c + half] = t.astype(BF16)

    for c in range(n_kn // MXU_DIM):
        sl = slice(c * MXU_DIM, (c + 1) * MXU_DIM)
        xk = kva[:, sl]
        yk = xk * lax.rsqrt(_group_ms(xk, bd_c) + EPS) * gak_ref[:, sl]
        for half in range(2):
            t = yk[:, half * LANE:(half + 1) * LANE]
            for k, tile in enumerate((t, pltpu.roll(t, HEAD_V, 1))):
                ka_ref[4 * c + 2 * half + k] = (jnp.where(low_half, tile, 0.0) + kr).astype(BF16)

    for kv in range(GQA_KV_HEADS):
        sl = slice(kv * MXU_DIM, (kv + 1) * MXU_DIM)
        xg = z[:, sl]
        yg = xg * lax.rsqrt(_group_ms(xg, bd_c) + EPS) * gbq_ref[:, sl]
        keep = low_half if kv == 0 else jnp.logical_not(low_half)
        for half in range(2):
            t = _rope(yg[:, half * LANE:(half + 1) * LANE], cos_b, sin_b, first_half)
            pair = (t, pltpu.roll(t, GQA_HD, 1)) if kv == 0 else (pltpu.roll(t, GQA_HD, 1), t)
            for k in range(2):
                head = GQA_GROUP * kv + 2 * half + k
                qb_ref[head] = jnp.where(keep, pair[k], 0.0).astype(BF16)


def _proj(h, lw, tabs, seq_rows):
    rows, d = h.shape
    tm = _div_tile(seq_rows, 640, LANE)
    per_seq = seq_rows // tm
    row_spec = lambda w: pl.BlockSpec((tm, w), lambda i: (i, 0))
    tab_spec = pl.BlockSpec((tm, LANE), lambda i: (i % per_seq, 0))
    head_spec = pl.BlockSpec((MLA_HEADS, tm, LANE), lambda i: (0, i, 0))
    vt_spec = lambda n: pl.BlockSpec((n, VT_ROWS, tm), lambda i: (0, 0, i))
    consts = [lw["g_mix"], lw["w_in_s"], lw["g_q"], lw["g_kv"], lw["w_uq_p"], lw["w_ukv_p"],
              tabs["bd_a"], tabs["bd_c"], tabs["bd_m"],
              lw["ga_q"], lw["ga_k"], lw["ga_kr"], lw["gb_q"], lw["gb_k"]]
    heads = jax.ShapeDtypeStruct((MLA_HEADS, rows, LANE), BF16)
    return pl.pallas_call(
        _proj_kernel,
        grid=(rows // tm,),
        in_specs=[row_spec(d)] + [_const_spec(c.shape) for c in consts] + [tab_spec] * 4,
        out_specs=[head_spec, head_spec, vt_spec(MLA_HEADS), head_spec, row_spec(LANE),
                   vt_spec(GQA_KV_HEADS)],
        out_shape=[heads, heads, jax.ShapeDtypeStruct((MLA_HEADS, VT_ROWS, rows), BF16), heads,
                   jax.ShapeDtypeStruct((rows, LANE), BF16),
                   jax.ShapeDtypeStruct((GQA_KV_HEADS, VT_ROWS, rows), BF16)],
        compiler_params=_params("parallel"),
        name="proj",
    )(h, *consts, tabs["cos_a"], tabs["sin_a"], tabs["cos_b"], tabs["sin_b"])


def _lane_ds(start, size):
    return pl.ds(start if isinstance(start, int) else pl.multiple_of(start, LANE), size)


def _col_max(s):
    return jnp.max(s, axis=0, keepdims=True)


def _flash_update(s, s_max, vt, carry):
    m, acc = carry
    m_new = jnp.maximum(m, s_max)
    p = jnp.exp2(s - m_new).astype(BF16)
    acc = jnp.exp2(m - m_new) * acc + _dot(vt, p)
    return m_new, acc


def _flash(qs, k_at, vt_at, s_refs, n_real, tk):
    n = len(qs)
    n_blk = n_real // tk
    assert n_real % tk == 0

    def scores(i, start, size):
        return lax.dot_general(k_at(i, start, size), qs[i], (((1,), (1,)), ((), ())),
                               preferred_element_type=F32)

    def issue(blk, s_buf):
        maxes = []
        for i in range(n):
            s = scores(i, blk * tk, tk)
            s_buf[i] = s
            maxes.append(_col_max(s))
        return maxes

    def stage(blk, parity, state, issue_next=True):
        carries, maxes = state
        next_maxes = issue(blk + 1, s_refs[1 - parity]) if issue_next else maxes
        carries = [_flash_update(s_refs[parity][i], maxes[i], vt_at(i, blk * tk, tk), carries[i])
                   for i in range(n)]
        return carries, next_maxes

    carries = [(jnp.full((1, q.shape[0]), NEG_BIG, F32), jnp.zeros((VT_ROWS, q.shape[0]), F32)) for q in qs]
    state = (carries, issue(0, s_refs[0]))
    looped = (n_blk - 1) // ATTN_UNROLL * ATTN_UNROLL if n_blk - 1 >= 2 * ATTN_UNROLL else 0
    if looped:
        def body(t, state):
            for k in range(ATTN_UNROLL):
                state = stage(t * ATTN_UNROLL + k, k % 2, state)
            return state
        state = lax.fori_loop(0, looped // ATTN_UNROLL, body, state)
    for blk in range(looped, n_blk):
        state = stage(blk, blk % 2, state, issue_next=blk + 1 < n_blk)
    carries, _ = state
    outs = []
    for i in range(n):
        s_meta = scores(i, n_real, META_PAD)
        is_key = lax.broadcasted_iota(jnp.int32, s_meta.shape, 0) < N_META
        s_meta = jnp.where(is_key, s_meta, NEG_BIG)
        _, acc = _flash_update(s_meta, _col_max(s_meta), vt_at(i, n_real, META_PAD), carries[i])
        outs.append(acc[:HEAD_V] / acc[HEAD_V:HEAD_V + 1])
    return outs


def _mla_kernel(q_ref, k_ref, vt_ref, o_ref, s0_ref, s1_ref, *, n_real, tk):
    qs = [q_ref[0, 0], q_ref[1, 0]]
    k_at = lambda i, s, n: k_ref[i, 0, _lane_ds(s, n), :]
    vt_at = lambda i, s, n: vt_ref[i, :, _lane_ds(s, n)]
    o0, o1 = _flash(qs, k_at, vt_at, (s0_ref, s1_ref), n_real, tk)
    o_ref[0] = jnp.concatenate([o0, o1], axis=0).T.astype(BF16)


def _kv_tile(n_real):
    return _div_tile(n_real, KV_BLOCK, LANE)


def _mla_attn(qa, ka, vta, batch, seq_rows):
    n_real = seq_rows - META_PAD
    tq = _div_tile(seq_rows, 1792, LANE)
    tk = _kv_tile(n_real)
    q4 = qa.reshape(MLA_HEADS, batch, seq_rows, LANE)
    k4 = ka.reshape(MLA_HEADS, batch, seq_rows, LANE)
    return pl.pallas_call(
        functools.partial(_mla_kernel, n_real=n_real, tk=tk),
        grid=(batch, MLA_HEADS // 2, seq_rows // tq),
        in_specs=[
            pl.BlockSpec((2, 1, tq, LANE), lambda b, j, i: (j, b, i, 0)),
            pl.BlockSpec((2, 1, seq_rows, LANE), lambda b, j, i: (j, b, 0, 0)),
            pl.BlockSpec((2, VT_ROWS, seq_rows), lambda b, j, i: (j, 0, b)),
        ],
        out_specs=pl.BlockSpec((1, tq, LANE), lambda b, j, i: (b, i, j)),
        out_shape=jax.ShapeDtypeStruct((batch, seq_rows, MLA_HEADS * MLA_V), BF16),
        scratch_shapes=[pltpu.VMEM((2, tk, tq), F32), pltpu.VMEM((2, tk, tq), F32)],
        compiler_params=_params("parallel", "parallel", "parallel"),
        name="mla_attn",
    )(q4, k4, vta).reshape(batch * seq_rows, MLA_HEADS * MLA_V)


def _gqa_kernel(q_ref, k_ref, vt_ref, o_ref, s0_ref, s1_ref, *, n_real, tk):
    g, _, tq, _ = q_ref.shape
    per = g // GQA_STREAMS
    qs = [q_ref[i * per:(i + 1) * per, 0].reshape(per * tq, LANE) for i in range(GQA_STREAMS)]
    k_at = lambda i, s, n: k_ref[0, _lane_ds(s, n), :]
    vt_at = lambda i, s, n: vt_ref[0, :, _lane_ds(s, n)]
    outs = _flash(qs, k_at, vt_at, (s0_ref, s1_ref), n_real, tk)
    heads = [o[:, i * tq:(i + 1) * tq] for o in outs for i in range(per)]
    o_ref[0] = jnp.concatenate(heads, axis=0).T.astype(BF16)


def _gqa_attn(qb, kb, vtb, batch, seq_rows):
    n_real = seq_rows - META_PAD
    per = GQA_GROUP // GQA_STREAMS
    tq = _div_tile(seq_rows, 704, LANE // per)
    tk = _kv_tile(n_real)
    q4 = qb.reshape(GQA_HEADS, batch, seq_rows, LANE)
    k3 = kb.reshape(batch, seq_rows, LANE)
    width = GQA_GROUP * GQA_HD
    s_buf = pltpu.VMEM((GQA_STREAMS, tk, per * tq), F32)
    return pl.pallas_call(
        functools.partial(_gqa_kernel, n_real=n_real, tk=tk),
        grid=(batch, GQA_KV_HEADS, seq_rows // tq),
        in_specs=[
            pl.BlockSpec((GQA_GROUP, 1, tq, LANE), lambda b, j, i: (j, b, i, 0)),
            pl.BlockSpec((1, seq_rows, LANE), lambda b, j, i: (b, 0, 0)),
            pl.BlockSpec((1, VT_ROWS, seq_rows), lambda b, j, i: (j, 0, b)),
        ],
        out_specs=pl.BlockSpec((1, tq, width), lambda b, j, i: (b, i, j)),
        out_shape=jax.ShapeDtypeStruct((batch, seq_rows, GQA_HEADS * GQA_HD), BF16),
        scratch_shapes=[s_buf, s_buf],
        compiler_params=_params("parallel", "parallel", "parallel"),
        name="gqa_attn",
    )(q4, k3, vtb).reshape(batch * seq_rows, GQA_HEADS * GQA_HD)


def _merge_kernel(h_ref, oa_ref, ob_ref, gmix_ref, wg_ref, bg_ref, wao_ref, wbo_ref, wout_ref, o_ref):
    h = h_ref[...]
    d = h.shape[-1]
    hn = _rms(h, gmix_ref[...]).astype(BF16)
    gates = jax.nn.sigmoid(_dot(hn, wg_ref[...]) + bg_ref[...])
    merged = gates[:, :d] * _dot(oa_ref[...], wao_ref[...]) + gates[:, d:] * _dot(ob_ref[...], wbo_ref[...])
    o_ref[...] = h + _dot(merged.astype(BF16), wout_ref[...])


def _merge(h, oa, ob, lw):
    rows, d = h.shape
    tm = _div_tile(rows, 640, LANE)
    row_spec = lambda w: pl.BlockSpec((tm, w), lambda i: (i, 0))
    consts = [lw["g_mix"], lw["w_gates"], lw["b_gate"], lw["w_mla_o"], lw["w_gqa_o"], lw["w_out"]]
    return pl.pallas_call(
        _merge_kernel,
        grid=(rows // tm,),
        in_specs=[row_spec(d), row_spec(oa.shape[1]), row_spec(ob.shape[1])]
        + [_const_spec(c.shape) for c in consts],
        out_specs=row_spec(d),
        out_shape=jax.ShapeDtypeStruct((rows, d), F32),
        compiler_params=_params("parallel"),
        name="merge",
    )(h, oa, ob, *consts)


def _block_diag_mean(groups):
    assert sum(groups) == MXU_DIM
    m = np.zeros((MXU_DIM, MXU_DIM), np.float32)
    c0 = 0
    for gsz in groups:
        m[c0:c0 + gsz, c0:c0 + gsz] = 1.0 / gsz
        c0 += gsz
    return jnp.asarray(m, BF16)


def _rope_angles(pos, dim):
    inv = 1.0 / (ROPE_THETA ** (jnp.arange(0, dim, 2, dtype=jnp.float32) / dim))
    return pos.astype(jnp.float32)[:, None] * inv[None, :]


def _tables(n_real):
    p = jnp.arange(n_real + META_PAD, dtype=jnp.int32)
    real = p < n_real
    meta = (p >= n_real) & (p < n_real + N_META)
    pos_1d = jnp.where(real, p + N_META, jnp.where(meta, p - n_real, 0))
    row = jnp.where(real, p // GRID_W, 0)
    col = jnp.where(real, p % GRID_W, 0)
    a1, ar, ac = _rope_angles(pos_1d, MLA_ROPE), _rope_angles(row, AXIAL_HALF), _rope_angles(col, AXIAL_HALF)
    ones = jnp.ones((p.shape[0], MLA_NOPE), F32)
    pad1 = jnp.ones((p.shape[0], LANE - MLA_QK), F32)
    cos_a = jnp.concatenate([ones, jnp.cos(a1), jnp.cos(a1), pad1], axis=1)
    sin_a = jnp.concatenate([0 * ones, -jnp.sin(a1), jnp.sin(a1), 0 * pad1], axis=1)
    cos_b = jnp.tile(jnp.concatenate([jnp.cos(ar), jnp.cos(ar), jnp.cos(ac), jnp.cos(ac)], axis=1), (1, 2))
    sin_b = jnp.tile(jnp.concatenate([-jnp.sin(ar), jnp.sin(ar), -jnp.sin(ac), jnp.sin(ac)], axis=1), (1, 2))
    return dict(
        cos_a=cos_a, sin_a=sin_a, cos_b=cos_b, sin_b=sin_b,
        bd_a=_block_diag_mean([MLA_NOPE, MLA_ROPE, LANE - MLA_QK] * 2),
        bd_c=_block_diag_mean([GQA_HD] * 4),
        bd_m=_block_diag_mean([MLA_NOPE, MLA_ROPE, LANE - MLA_QK, GQA_HD, GQA_HD]),
    )


def _prep_layer(ffn1_norm, ffn1_w13, ffn1_w2, mix_norm, w_in, b_gate, q_a_norm, kv_a_norm, w_uq, w_ukv,
                mla_q_norm, mla_k_norm, gqa_q_norm, gqa_k_norm, w_mla_o, w_gqa_o, w_out,
                ffn2_norm, ffn2_w13, ffn2_w2, chunks):
    d = w_in.shape[0]
    d_ff = ffn1_w2.shape[0]

    def w13r(w13):
        pieces = []
        for c0, w in chunks:
            pieces += [w13[:, c0:c0 + w], w13[:, d_ff + c0:d_ff + c0 + w]]
        return jnp.concatenate(pieces, axis=1).astype(BF16)

    cuts = np.cumsum([0, Q_LORA, KV_LORA, MLA_ROPE, GQA_HEADS * GQA_HD, GQA_KV_HEADS * GQA_HD,
                      GQA_KV_HEADS * GQA_HD]).tolist()
    w_cq, w_ckv, w_kr, w_qg, w_kg, w_vg = (w_in[:, cuts[i]:cuts[i + 1]] for i in range(6))
    w_gates = w_in[:, cuts[6]:]
    kr_cols = [jnp.zeros((d, MLA_NOPE), F32), w_kr, jnp.zeros((d, LANE - MLA_QK), F32)]
    w_in_s = jnp.concatenate([w_qg] + kr_cols + [w_kg, w_vg, w_cq, w_ckv], axis=1).astype(BF16)
    assert w_in_s.shape[1] == _C_END

    uq_cols, ukv_cols, v_cols = [], [], []
    for hd in range(MLA_HEADS):
        uq_cols += [w_uq[:, hd * MLA_QK:(hd + 1) * MLA_QK], jnp.zeros((Q_LORA, LANE - MLA_QK), F32)]
        base = hd * (MLA_NOPE + MLA_V)
        ukv_cols += [w_ukv[:, base:base + MLA_NOPE]]
        v_cols += [w_ukv[:, base + MLA_NOPE:base + MLA_NOPE + MLA_V]]
    w_uq_p = jnp.concatenate(uq_cols, axis=1).astype(BF16)
    w_ukv_p = jnp.concatenate(ukv_cols + v_cols, axis=1).astype(BF16)

    zq = jnp.zeros((LANE - MLA_QK,), F32)
    ga_q = jnp.tile(jnp.concatenate([mla_q_norm, zq]), MLA_HEADS) * (MLA_QK ** -0.5 * LOG2E)
    ga_k = jnp.tile(mla_k_norm[:MLA_NOPE], MLA_HEADS)
    ga_kr = jnp.concatenate([jnp.zeros((MLA_NOPE,), F32), mla_k_norm[MLA_NOPE:], zq])
    gb_q = jnp.tile(gqa_q_norm, GQA_HEADS) * (GQA_HD ** -0.5 * LOG2E)
    gb_k = jnp.tile(gqa_k_norm, GQA_KV_HEADS)
    row = lambda v: v.reshape(1, -1).astype(F32)
    return dict(
        g_ffn1=row(ffn1_norm), w13r_1=w13r(ffn1_w13), w2_1=ffn1_w2.astype(BF16),
        g_ffn2=row(ffn2_norm), w13r_2=w13r(ffn2_w13), w2_2=ffn2_w2.astype(BF16),
        g_mix=row(mix_norm), w_in_s=w_in_s, g_q=row(q_a_norm), g_kv=row(kv_a_norm),
        w_uq_p=w_uq_p, w_ukv_p=w_ukv_p,
        ga_q=row(ga_q), ga_k=row(ga_k), ga_kr=row(ga_kr), gb_q=row(gb_q), gb_k=row(gb_k),
        w_gates=w_gates.astype(BF16), b_gate=row(b_gate),
        w_mla_o=w_mla_o.astype(BF16), w_gqa_o=w_gqa_o.astype(BF16), w_out=w_out.astype(BF16),
    )


def _trunk(x, meta_tokens, layers, chunks):
    batch, n_real, d = x.shape
    assert n_real % GRID_W == 0 and n_real % (8 * LANE) == 0
    seq_rows = n_real + META_PAD
    tail = jnp.concatenate([meta_tokens.astype(x.dtype), jnp.zeros((META_PAD - N_META, d), x.dtype)], axis=0)
    h = jnp.concatenate([x, jnp.broadcast_to(tail[None], (batch, META_PAD, d))], axis=1)
    h = h.reshape(batch * seq_rows, d)
    tabs = _tables(n_real)
    for idx, lw in enumerate(layers):
        h = _ffn(h, lw["g_ffn1"], lw["w13r_1"], lw["w2_1"], chunks)
        qa, ka, vta, qb, kb, vtb = _proj(h, lw, tabs, seq_rows)
        oa = _mla_attn(qa, ka, vta, batch, seq_rows)
        ob = _gqa_attn(qb, kb, vtb, batch, seq_rows)
        h = _merge(h, oa, ob, lw)
        real_only = (batch, seq_rows, n_real) if idx == len(layers) - 1 else None
        h = _ffn(h, lw["g_ffn2"], lw["w13r_2"], lw["w2_2"], chunks, real_only)
    return h


def kernel(x_prompt, x_sample, meta_tokens, ffn1_norm, ffn1_w13, ffn1_w2, mix_norm, w_in, b_gate, q_a_norm,
           kv_a_norm, w_uq, w_ukv, mla_q_norm, mla_k_norm, gqa_q_norm, gqa_k_norm, w_mla_o, w_gqa_o, w_out,
           ffn2_norm, ffn2_w13, ffn2_w2):
    per_layer = (ffn1_norm, ffn1_w13, ffn1_w2, mix_norm, w_in, b_gate, q_a_norm, kv_a_norm, w_uq, w_ukv,
                 mla_q_norm, mla_k_norm, gqa_q_norm, gqa_k_norm, w_mla_o, w_gqa_o, w_out,
                 ffn2_norm, ffn2_w13, ffn2_w2)
    chunks = _ff_chunks(ffn1_w2.shape[1])
    layers = [_prep_layer(*(p[l] for p in per_layer), chunks=chunks) for l in range(ffn1_norm.shape[0])]
    return (_trunk(x_prompt, meta_tokens, layers, chunks), _trunk(x_sample, meta_tokens, layers, chunks))
```

```python
import functools
import math

import numpy as np
import jax
import jax.numpy as jnp
from jax import lax
from jax.experimental import pallas as pl
from jax.experimental.pallas import tpu as pltpu

N_META = 16
GRID_W = 64
EPS = 1e-6
ROPE_THETA = 10000.0
MLA_HEADS = 8
MLA_NOPE = 64
MLA_ROPE = 32
MLA_QK = MLA_NOPE + MLA_ROPE
MLA_V = 64
Q_LORA = 384
KV_LORA = 256
GQA_HEADS = 8
GQA_KV_HEADS = 2
GQA_GROUP = GQA_HEADS // GQA_KV_HEADS
GQA_HD = 64
AXIAL_HALF = GQA_HD // 2
ROPE_PAIR = 16
HEAD_V = 64

LANE = 128
MXU_DIM = 256
BF16_ROWS = 16
VMEM_LIMIT = 56 * 1024 * 1024

META_PAD = LANE
VT_ROWS = HEAD_V + BF16_ROWS
KV_BLOCK = 512
ATTN_UNROLL = 4
MAX_RISE = 64.0
GQA_STREAMS = 2
LOG2E = math.log2(math.e)
NEG_BIG = -1e30

BF16 = jnp.bfloat16
F32 = jnp.float32


def _div_tile(n, target, mult):
    best = None
    for t in range(mult, min(n, target) + 1, mult):
        if n % t == 0:
            best = t
    assert best is not None, (n, target, mult)
    return best


def _ff_chunks(d_ff):
    chunks, c0 = [], 0
    while c0 < d_ff:
        w = min(3 * MXU_DIM, d_ff - c0)
        chunks.append((c0, w))
        c0 += w
    return tuple(chunks)


def _dot(a, b):
    return jnp.dot(a, b, preferred_element_type=F32)


def _rms(x, g):
    return x * lax.rsqrt(jnp.mean(x * x, axis=-1, keepdims=True) + EPS) * g


def _const_spec(shape):
    nd = len(shape)
    return pl.BlockSpec(shape, lambda *_: (0,) * nd, pipeline_mode=pl.Buffered(1))


def _params(*sem):
    return pltpu.CompilerParams(dimension_semantics=sem, vmem_limit_bytes=VMEM_LIMIT)


def _ffn_kernel(h_ref, g_ref, w13_ref, w2_ref, o_ref, *, chunks):
    h = h_ref[...]
    xn = _rms(h, g_ref[...]).astype(BF16)
    acc = None
    off = 0
    for c0, w in chunks:
        gu = _dot(xn, w13_ref[:, off:off + 2 * w])
        gate, up = gu[:, :w], gu[:, w:]
        a = (gate * jax.nn.sigmoid(gate) * up).astype(BF16)
        part = _dot(a, w2_ref[c0:c0 + w, :])
        acc = part if acc is None else acc + part
        off += 2 * w
    o_ref[...] = h + 0.5 * acc


def _ffn(h, g, w13r, w2, chunks, real_only=None):
    rows, d = h.shape
    if real_only is None:
        tm = _div_tile(rows, 640, LANE)
        grid, sem = (rows // tm,), ("parallel",)
        spec = pl.BlockSpec((tm, d), lambda i: (i, 0))
        out_shape = jax.ShapeDtypeStruct((rows, d), F32)
    else:
        batch, seq_rows, n_real = real_only
        tm = _div_tile(n_real, 640, LANE)
        h = h.reshape(batch, seq_rows, d)
        grid, sem = (batch, n_real // tm), ("parallel", "parallel")
        spec = pl.BlockSpec((None, tm, d), lambda b, i: (b, i, 0))
        out_shape = jax.ShapeDtypeStruct((batch, n_real, d), F32)
    return pl.pallas_call(
        functools.partial(_ffn_kernel, chunks=chunks),
        grid=grid,
        in_specs=[spec, _const_spec(g.shape), _const_spec(w13r.shape), _const_spec(w2.shape)],
        out_specs=spec,
        out_shape=out_shape,
        compiler_params=_params(*sem),
        name="ffn",
    )(h, g, w13r, w2)


_C_QG = 0
_C_KR = 512
_C_KG = 640
_C_VG = 768
_C_CQ = 896
_C_CKV = 1280
_C_END = 1536


def _group_ms(x, bd):
    return _dot((x * x).astype(BF16), bd)


def _rope(y, cos, sin, first_half):
    partner = jnp.where(first_half, pltpu.roll(y, LANE - ROPE_PAIR, 1), pltpu.roll(y, ROPE_PAIR, 1))
    return y * cos + partner * sin


def _store_vt(vt_ref, head0, v_pair, ones_tile):
    vt = v_pair.T
    for k in range(2):
        vt_ref[head0 + k, :HEAD_V, :] = vt[k * HEAD_V:(k + 1) * HEAD_V].astype(BF16)
        vt_ref[head0 + k, HEAD_V:, :] = ones_tile


def _proj_kernel(h_ref, gmix_ref, win_ref, gq_ref, gkv_ref, wuq_ref, wukv_ref,
                 bd_a_ref, bd_c_ref, bd_m_ref, gaq_ref, gak_ref, gakr_ref, gbq_ref, gbk_ref,
                 cos_a_ref, sin_a_ref, cos_b_ref, sin_b_ref,
                 qa_ref, ka_ref, vta_ref, qb_ref, kb_ref, vtb_ref):
    tm = h_ref.shape[0]
    hn = _rms(h_ref[...], gmix_ref[...]).astype(BF16)
    z = _dot(hn, win_ref[...])
    cq = _rms(z[:, _C_CQ:_C_CKV], gq_ref[...]).astype(BF16)
    ckv = _rms(z[:, _C_CKV:_C_END], gkv_ref[...]).astype(BF16)
    qa = _dot(cq, wuq_ref[...])
    kva = _dot(ckv, wukv_ref[...])
    n_kn = MLA_HEADS * MLA_NOPE

    lane = lax.broadcasted_iota(jnp.int32, (tm, LANE), 1)
    first_half = (lane % (2 * ROPE_PAIR)) < ROPE_PAIR
    low_half = lane < HEAD_V
    cos_a, sin_a = cos_a_ref[...], sin_a_ref[...]
    cos_b, sin_b = cos_b_ref[...], sin_b_ref[...]
    bd_a, bd_c, bd_m = bd_a_ref[...], bd_c_ref[...], bd_m_ref[...]
    ones_tile = (lax.broadcasted_iota(jnp.int32, (BF16_ROWS, tm), 0) == 0).astype(F32).astype(BF16)

    x = z[:, _C_KR:_C_VG]
    y = x * lax.rsqrt(_group_ms(x, bd_m) + EPS)
    kr = _rope(y[:, :LANE] * gakr_ref[...], cos_a, sin_a, first_half)
    kb_ref[...] = _rope(y[:, LANE:] * gbk_ref[...], cos_b, sin_b, first_half).astype(BF16)
    _store_vt(vtb_ref, 0, z[:, _C_VG:_C_CQ], ones_tile)
    for c in range(MLA_HEADS // 2):
        v0 = n_kn + c * LANE
        _store_vt(vta_ref, 2 * c, kva[:, v0:v0 + LANE], ones_tile)

    for c in range(MLA_HEADS * LANE // MXU_DIM):
        sl = slice(c * MXU_DIM, (c + 1) * MXU_DIM)
        x = qa[:, sl]
        y = x * lax.rsqrt(_group_ms(x, bd_a) + EPS) * gaq_ref[:, sl]
        for half in range(2):
            t = _rope(y[:, half * LANE:(half + 1) * LANE], cos_a, sin_a, first_half)
            qa_ref[2 * c + half] = t.astype(BF16)

    for c in range(n_kn // MXU_DIM):
        sl = slice(c * MXU_DIM, (c + 1) * MXU_DIM)
        xk = kva[:, sl]
        yk = xk * lax.rsqrt(_group_ms(xk, bd_c) + EPS) * gak_ref[:, sl]
        for half in range(2):
            t = yk[:, half * LANE:(half + 1) * LANE]
            for k, tile in enumerate((t, pltpu.roll(t, HEAD_V, 1))):
                ka_ref[4 * c + 2 * half + k] = (jnp.where(low_half, tile, 0.0) + kr).astype(BF16)

    for kv in range(GQA_KV_HEADS):
        sl = slice(kv * MXU_DIM, (kv + 1) * MXU_DIM)
        xg = z[:, sl]
        yg = xg * lax.rsqrt(_group_ms(xg, bd_c) + EPS) * gbq_ref[:, sl]
        keep = low_half if kv == 0 else jnp.logical_not(low_half)
        for half in range(2):
            t = _rope(yg[:, half * LANE:(half + 1) * LANE], cos_b, sin_b, first_half)
            pair = (t, pltpu.roll(t, GQA_HD, 1)) if kv == 0 else (pltpu.roll(t, GQA_HD, 1), t)
            for k in range(2):
                head = GQA_GROUP * kv + 2 * half + k
                qb_ref[head] = jnp.where(keep, pair[k], 0.0).astype(BF16)


def _proj(h, lw, tabs, seq_rows):
    rows, d = h.shape
    tm = _div_tile(seq_rows, 640, LANE)
    per_seq = seq_rows // tm
    row_spec = lambda w: pl.BlockSpec((tm, w), lambda i: (i, 0))
    tab_spec = pl.BlockSpec((tm, LANE), lambda i: (i % per_seq, 0))
    head_spec = pl.BlockSpec((MLA_HEADS, tm, LANE), lambda i: (0, i, 0))
    vt_spec = lambda n: pl.BlockSpec((n, VT_ROWS, tm), lambda i: (0, 0, i))
    consts = [lw["g_mix"], lw["w_in_s"], lw["g_q"], lw["g_kv"], lw["w_uq_p"], lw["w_ukv_p"],
              tabs["bd_a"], tabs["bd_c"], tabs["bd_m"],
              lw["ga_q"], lw["ga_k"], lw["ga_kr"], lw["gb_q"], lw["gb_k"]]
    heads = jax.ShapeDtypeStruct((MLA_HEADS, rows, LANE), BF16)
    return pl.pallas_call(
        _proj_kernel,
        grid=(rows // tm,),
        in_specs=[row_spec(d)] + [_const_spec(c.shape) for c in consts] + [tab_spec] * 4,
        out_specs=[head_spec, head_spec, vt_spec(MLA_HEADS), head_spec, row_spec(LANE),
                   vt_spec(GQA_KV_HEADS)],
        out_shape=[heads, heads, jax.ShapeDtypeStruct((MLA_HEADS, VT_ROWS, rows), BF16), heads,
                   jax.ShapeDtypeStruct((rows, LANE), BF16),
                   jax.ShapeDtypeStruct((GQA_KV_HEADS, VT_ROWS, rows), BF16)],
        compiler_params=_params("parallel"),
        name="proj",
    )(h, *consts, tabs["cos_a"], tabs["sin_a"], tabs["cos_b"], tabs["sin_b"])


def _lane_ds(start, size):
    return pl.ds(start if isinstance(start, int) else pl.multiple_of(start, LANE), size)


def _col_max(s):
    return jnp.max(s, axis=0, keepdims=True)


def _flash_update(s, s_max, vt, carry):
    m, acc = carry
    m_new = jnp.maximum(m, s_max)
    p = jnp.exp2(s - m_new).astype(BF16)
    acc = jnp.exp2(m - m_new) * acc + _dot(vt, p)
    return m_new, acc


def _scores_fn(qs, k_at):
    def scores(i, start, size):
        return lax.dot_general(k_at(i, start, size), qs[i], (((1,), (1,)), ((), ())),
                               preferred_element_type=F32)
    return scores


def _looped_blocks(n_blocks, unroll):
    return n_blocks // unroll * unroll if n_blocks >= 2 * unroll else 0


def _meta_mask(nq):
    return lax.broadcasted_iota(jnp.int32, (META_PAD, nq), 0) < N_META


def _flash_one_pass(qs, k_at, vt_at, n_real, tk):
    n = len(qs)
    n_blk = n_real // tk
    assert n_real % tk == 0
    scores = _scores_fn(qs, k_at)

    def fold(i, start, size, carry, mask=None):
        m, acc, rise = carry
        s = scores(i, start, size)
        if mask is not None:
            s = jnp.where(mask, s, NEG_BIG)
        s_max = _col_max(s)
        acc = acc + _dot(vt_at(i, start, size), jnp.exp2(s - m).astype(BF16))
        m_new = jnp.maximum(m, s_max)
        return m_new, acc * jnp.exp2(m - m_new), jnp.maximum(rise, s_max - m)

    def sweep(first, count, carries):
        for k in range(count):
            carries = [fold(i, (first + k) * tk, tk, carries[i]) for i in range(n)]
        return carries

    carries = []
    for i in range(n):
        s = scores(i, 0, tk)
        m = _col_max(s)
        carries.append((m, _dot(vt_at(i, 0, tk), jnp.exp2(s - m).astype(BF16)), jnp.zeros_like(m)))
    looped = _looped_blocks(n_blk - 1, ATTN_UNROLL)
    if looped:
        carries = lax.fori_loop(0, looped // ATTN_UNROLL,
                                lambda t, c: sweep(1 + t * ATTN_UNROLL, ATTN_UNROLL, c), carries)
    carries = sweep(1 + looped, n_blk - 1 - looped, carries)
    outs, rise = [], None
    for i in range(n):
        _, acc, r = fold(i, n_real, META_PAD, carries[i], _meta_mask(qs[i].shape[0]))
        outs.append(acc[:HEAD_V] / acc[HEAD_V:HEAD_V + 1])
        rise = jnp.max(r) if rise is None else jnp.maximum(rise, jnp.max(r))
    return outs, rise


def _attend(qs, k_at, vt_at, s_refs, n_real, tk, write):
    outs, rise = _flash_one_pass(qs, k_at, vt_at, n_real, tk)
    write(outs)

    @pl.when(rise > MAX_RISE)
    def _():
        write(_flash(qs, k_at, vt_at, s_refs, n_real, tk))


def _flash(qs, k_at, vt_at, s_refs, n_real, tk):
    n = len(qs)
    n_blk = n_real // tk
    assert n_real % tk == 0
    scores = _scores_fn(qs, k_at)

    def issue(blk, s_buf):
        maxes = []
        for i in range(n):
            s = scores(i, blk * tk, tk)
            s_buf[i] = s
            maxes.append(_col_max(s))
        return maxes

    def stage(blk, parity, state, issue_next=True):
        carries, maxes = state
        next_maxes = issue(blk + 1, s_refs[1 - parity]) if issue_next else maxes
        carries = [_flash_update(s_refs[parity][i], maxes[i], vt_at(i, blk * tk, tk), carries[i])
                   for i in range(n)]
        return carries, next_maxes

    carries = [(jnp.full((1, q.shape[0]), NEG_BIG, F32), jnp.zeros((VT_ROWS, q.shape[0]), F32)) for q in qs]
    state = (carries, issue(0, s_refs[0]))
    looped = _looped_blocks(n_blk - 1, 2)
    if looped:
        def body(t, state):
            return stage(2 * t + 1, 1, stage(2 * t, 0, state))
        state = lax.fori_loop(0, looped // 2, body, state)
    for blk in range(looped, n_blk):
        state = stage(blk, blk % 2, state, issue_next=blk + 1 < n_blk)
    carries, _ = state
    outs = []
    for i in range(n):
        s_meta = jnp.where(_meta_mask(qs[i].shape[0]), scores(i, n_real, META_PAD), NEG_BIG)
        _, acc = _flash_update(s_meta, _col_max(s_meta), vt_at(i, n_real, META_PAD), carries[i])
        outs.append(acc[:HEAD_V] / acc[HEAD_V:HEAD_V + 1])
    return outs


def _mla_kernel(q_ref, k_ref, vt_ref, o_ref, s0_ref, s1_ref, *, n_real, tk):
    qs = [q_ref[0, 0], q_ref[1, 0]]
    k_at = lambda i, s, n: k_ref[i, 0, _lane_ds(s, n), :]
    vt_at = lambda i, s, n: vt_ref[i, :, _lane_ds(s, n)]

    def write(outs):
        o_ref[0] = jnp.concatenate(outs, axis=0).T.astype(BF16)

    _attend(qs, k_at, vt_at, (s0_ref, s1_ref), n_real, tk, write)


def _kv_tile(n_real):
    return _div_tile(n_real, KV_BLOCK, LANE)


def _mla_attn(qa, ka, vta, batch, seq_rows):
    n_real = seq_rows - META_PAD
    tq = _div_tile(seq_rows, 1792, LANE)
    tk = _kv_tile(n_real)
    q4 = qa.reshape(MLA_HEADS, batch, seq_rows, LANE)
    k4 = ka.reshape(MLA_HEADS, batch, seq_rows, LANE)
    return pl.pallas_call(
        functools.partial(_mla_kernel, n_real=n_real, tk=tk),
        grid=(batch, MLA_HEADS // 2, seq_rows // tq),
        in_specs=[
            pl.BlockSpec((2, 1, tq, LANE), lambda b, j, i: (j, b, i, 0)),
            pl.BlockSpec((2, 1, seq_rows, LANE), lambda b, j, i: (j, b, 0, 0)),
            pl.BlockSpec((2, VT_ROWS, seq_rows), lambda b, j, i: (j, 0, b)),
        ],
        out_specs=pl.BlockSpec((1, tq, LANE), lambda b, j, i: (b, i, j)),
        out_shape=jax.ShapeDtypeStruct((batch, seq_rows, MLA_HEADS * MLA_V), BF16),
        scratch_shapes=[pltpu.VMEM((2, tk, tq), F32), pltpu.VMEM((2, tk, tq), F32)],
        compiler_params=_params("parallel", "parallel", "parallel"),
        name="mla_attn",
    )(q4, k4, vta).reshape(batch * seq_rows, MLA_HEADS * MLA_V)


def _gqa_kernel(q_ref, k_ref, vt_ref, o_ref, s0_ref, s1_ref, *, n_real, tk):
    g, _, tq, _ = q_ref.shape
    per = g // GQA_STREAMS
    qs = [q_ref[i * per:(i + 1) * per, 0].reshape(per * tq, LANE) for i in range(GQA_STREAMS)]
    k_at = lambda i, s, n: k_ref[0, _lane_ds(s, n), :]
    vt_at = lambda i, s, n: vt_ref[0, :, _lane_ds(s, n)]

    def write(outs):
        heads = [o[:, i * tq:(i + 1) * tq] for o in outs for i in range(per)]
        o_ref[0] = jnp.concatenate(heads, axis=0).T.astype(BF16)

    _attend(qs, k_at, vt_at, (s0_ref, s1_ref), n_real, tk, write)


def _gqa_attn(qb, kb, vtb, batch, seq_rows):
    n_real = seq_rows - META_PAD
    per = GQA_GROUP // GQA_STREAMS
    tq = _div_tile(seq_rows, 704, LANE // per)
    tk = _kv_tile(n_real)
    q4 = qb.reshape(GQA_HEADS, batch, seq_rows, LANE)
    k3 = kb.reshape(batch, seq_rows, LANE)
    width = GQA_GROUP * GQA_HD
    s_buf = pltpu.VMEM((GQA_STREAMS, tk, per * tq), F32)
    return pl.pallas_call(
        functools.partial(_gqa_kernel, n_real=n_real, tk=tk),
        grid=(batch, GQA_KV_HEADS, seq_rows // tq),
        in_specs=[
            pl.BlockSpec((GQA_GROUP, 1, tq, LANE), lambda b, j, i: (j, b, i, 0)),
            pl.BlockSpec((1, seq_rows, LANE), lambda b, j, i: (b, 0, 0)),
            pl.BlockSpec((1, VT_ROWS, seq_rows), lambda b, j, i: (j, 0, b)),
        ],
        out_specs=pl.BlockSpec((1, tq, width), lambda b, j, i: (b, i, j)),
        out_shape=jax.ShapeDtypeStruct((batch, seq_rows, GQA_HEADS * GQA_HD), BF16),
        scratch_shapes=[s_buf, s_buf],
        compiler_params=_params("parallel", "parallel", "parallel"),
        name="gqa_attn",
    )(q4, k3, vtb).reshape(batch * seq_rows, GQA_HEADS * GQA_HD)


def _merge_kernel(h_ref, oa_ref, ob_ref, gmix_ref, wg_ref, bg_ref, wao_ref, wbo_ref, wout_ref, o_ref):
    h = h_ref[...]
    d = h.shape[-1]
    hn = _rms(h, gmix_ref[...]).astype(BF16)
    gates = jax.nn.sigmoid(_dot(hn, wg_ref[...]) + bg_ref[...])
    merged = gates[:, :d] * _dot(oa_ref[...], wao_ref[...]) + gates[:, d:] * _dot(ob_ref[...], wbo_ref[...])
    o_ref[...] = h + _dot(merged.astype(BF16), wout_ref[...])


def _merge(h, oa, ob, lw):
    rows, d = h.shape
    tm = _div_tile(rows, 640, LANE)
    row_spec = lambda w: pl.BlockSpec((tm, w), lambda i: (i, 0))
    consts = [lw["g_mix"], lw["w_gates"], lw["b_gate"], lw["w_mla_o"], lw["w_gqa_o"], lw["w_out"]]
    return pl.pallas_call(
        _merge_kernel,
        grid=(rows // tm,),
        in_specs=[row_spec(d), row_spec(oa.shape[1]), row_spec(ob.shape[1])]
        + [_const_spec(c.shape) for c in consts],
        out_specs=row_spec(d),
        out_shape=jax.ShapeDtypeStruct((rows, d), F32),
        compiler_params=_params("parallel"),
        name="merge",
    )(h, oa, ob, *consts)


def _block_diag_mean(groups):
    assert sum(groups) == MXU_DIM
    m = np.zeros((MXU_DIM, MXU_DIM), np.float32)
    c0 = 0
    for gsz in groups:
        m[c0:c0 + gsz, c0:c0 + gsz] = 1.0 / gsz
        c0 += gsz
    return jnp.asarray(m, BF16)


def _rope_angles(pos, dim):
    inv = 1.0 / (ROPE_THETA ** (jnp.arange(0, dim, 2, dtype=jnp.float32) / dim))
    return pos.astype(jnp.float32)[:, None] * inv[None, :]


def _tables(n_real):
    p = jnp.arange(n_real + META_PAD, dtype=jnp.int32)
    real = p < n_real
    meta = (p >= n_real) & (p < n_real + N_META)
    pos_1d = jnp.where(real, p + N_META, jnp.where(meta, p - n_real, 0))
    row = jnp.where(real, p // GRID_W, 0)
    col = jnp.where(real, p % GRID_W, 0)
    a1, ar, ac = _rope_angles(pos_1d, MLA_ROPE), _rope_angles(row, AXIAL_HALF), _rope_angles(col, AXIAL_HALF)
    ones = jnp.ones((p.shape[0], MLA_NOPE), F32)
    pad1 = jnp.ones((p.shape[0], LANE - MLA_QK), F32)
    cos_a = jnp.concatenate([ones, jnp.cos(a1), jnp.cos(a1), pad1], axis=1)
    sin_a = jnp.concatenate([0 * ones, -jnp.sin(a1), jnp.sin(a1), 0 * pad1], axis=1)
    cos_b = jnp.tile(jnp.concatenate([jnp.cos(ar), jnp.cos(ar), jnp.cos(ac), jnp.cos(ac)], axis=1), (1, 2))
    sin_b = jnp.tile(jnp.concatenate([-jnp.sin(ar), jnp.sin(ar), -jnp.sin(ac), jnp.sin(ac)], axis=1), (1, 2))
    return dict(
        cos_a=cos_a, sin_a=sin_a, cos_b=cos_b, sin_b=sin_b,
        bd_a=_block_diag_mean([MLA_NOPE, MLA_ROPE, LANE - MLA_QK] * 2),
        bd_c=_block_diag_mean([GQA_HD] * 4),
        bd_m=_block_diag_mean([MLA_NOPE, MLA_ROPE, LANE - MLA_QK, GQA_HD, GQA_HD]),
    )


def _prep_layer(ffn1_norm, ffn1_w13, ffn1_w2, mix_norm, w_in, b_gate, q_a_norm, kv_a_norm, w_uq, w_ukv,
                mla_q_norm, mla_k_norm, gqa_q_norm, gqa_k_norm, w_mla_o, w_gqa_o, w_out,
                ffn2_norm, ffn2_w13, ffn2_w2, chunks):
    d = w_in.shape[0]
    d_ff = ffn1_w2.shape[0]

    def w13r(w13):
        pieces = []
        for c0, w in chunks:
            pieces += [w13[:, c0:c0 + w], w13[:, d_ff + c0:d_ff + c0 + w]]
        return jnp.concatenate(pieces, axis=1).astype(BF16)

    cuts = np.cumsum([0, Q_LORA, KV_LORA, MLA_ROPE, GQA_HEADS * GQA_HD, GQA_KV_HEADS * GQA_HD,
                      GQA_KV_HEADS * GQA_HD]).tolist()
    w_cq, w_ckv, w_kr, w_qg, w_kg, w_vg = (w_in[:, cuts[i]:cuts[i + 1]] for i in range(6))
    w_gates = w_in[:, cuts[6]:]
    kr_cols = [jnp.zeros((d, MLA_NOPE), F32), w_kr, jnp.zeros((d, LANE - MLA_QK), F32)]
    w_in_s = jnp.concatenate([w_qg] + kr_cols + [w_kg, w_vg, w_cq, w_ckv], axis=1).astype(BF16)
    assert w_in_s.shape[1] == _C_END

    uq_cols, ukv_cols, v_cols = [], [], []
    for hd in range(MLA_HEADS):
        uq_cols += [w_uq[:, hd * MLA_QK:(hd + 1) * MLA_QK], jnp.zeros((Q_LORA, LANE - MLA_QK), F32)]
        base = hd * (MLA_NOPE + MLA_V)
        ukv_cols += [w_ukv[:, base:base + MLA_NOPE]]
        v_cols += [w_ukv[:, base + MLA_NOPE:base + MLA_NOPE + MLA_V]]
    w_uq_p = jnp.concatenate(uq_cols, axis=1).astype(BF16)
    w_ukv_p = jnp.concatenate(ukv_cols + v_cols, axis=1).astype(BF16)

    zq = jnp.zeros((LANE - MLA_QK,), F32)
    ga_q = jnp.tile(jnp.concatenate([mla_q_norm, zq]), MLA_HEADS) * (MLA_QK ** -0.5 * LOG2E)
    ga_k = jnp.tile(mla_k_norm[:MLA_NOPE], MLA_HEADS)
    ga_kr = jnp.concatenate([jnp.zeros((MLA_NOPE,), F32), mla_k_norm[MLA_NOPE:], zq])
    gb_q = jnp.tile(gqa_q_norm, GQA_HEADS) * (GQA_HD ** -0.5 * LOG2E)
    gb_k = jnp.tile(gqa_k_norm, GQA_KV_HEADS)
    row = lambda v: v.reshape(1, -1).astype(F32)
    return dict(
        g_ffn1=row(ffn1_norm), w13r_1=w13r(ffn1_w13), w2_1=ffn1_w2.astype(BF16),
        g_ffn2=row(ffn2_norm), w13r_2=w13r(ffn2_w13), w2_2=ffn2_w2.astype(BF16),
        g_mix=row(mix_norm), w_in_s=w_in_s, g_q=row(q_a_norm), g_kv=row(kv_a_norm),
        w_uq_p=w_uq_p, w_ukv_p=w_ukv_p,
        ga_q=row(ga_q), ga_k=row(ga_k), ga_kr=row(ga_kr), gb_q=row(gb_q), gb_k=row(gb_k),
        w_gates=w_gates.astype(BF16), b_gate=row(b_gate),
        w_mla_o=w_mla_o.astype(BF16), w_gqa_o=w_gqa_o.astype(BF16), w_out=w_out.astype(BF16),
    )


def _trunk(x, meta_tokens, layers, chunks):
    batch, n_real, d = x.shape
    assert n_real % GRID_W == 0 and n_real % (8 * LANE) == 0
    seq_rows = n_real + META_PAD
    tail = jnp.concatenate([meta_tokens.astype(x.dtype), jnp.zeros((META_PAD - N_META, d), x.dtype)], axis=0)
    h = jnp.concatenate([x, jnp.broadcast_to(tail[None], (batch, META_PAD, d))], axis=1)
    h = h.reshape(batch * seq_rows, d)
    tabs = _tables(n_real)
    for idx, lw in enumerate(layers):
        h = _ffn(h, lw["g_ffn1"], lw["w13r_1"], lw["w2_1"], chunks)
        qa, ka, vta, qb, kb, vtb = _proj(h, lw, tabs, seq_rows)
        oa = _mla_attn(qa, ka, vta, batch, seq_rows)
        ob = _gqa_attn(qb, kb, vtb, batch, seq_rows)
        h = _merge(h, oa, ob, lw)
        real_only = (batch, seq_rows, n_real) if idx == len(layers) - 1 else None
        h = _ffn(h, lw["g_ffn2"], lw["w13r_2"], lw["w2_2"], chunks, real_only)
    return h


def kernel(x_prompt, x_sample, meta_tokens, ffn1_norm, ffn1_w13, ffn1_w2, mix_norm, w_in, b_gate, q_a_norm,
           kv_a_norm, w_uq, w_ukv, mla_q_norm, mla_k_norm, gqa_q_norm, gqa_k_norm, w_mla_o, w_gqa_o, w_out,
           ffn2_norm, ffn2_w13, ffn2_w2):
    per_layer = (ffn1_norm, ffn1_w13, ffn1_w2, mix_norm, w_in, b_gate, q_a_norm, kv_a_norm, w_uq, w_ukv,
                 mla_q_norm, mla_k_norm, gqa_q_norm, gqa_k_norm, w_mla_o, w_gqa_o, w_out,
                 ffn2_norm, ffn2_w13, ffn2_w2)
    chunks = _ff_chunks(ffn1_w2.shape[1])
    layers = [_prep_layer(*(p[l] for p in per_layer), chunks=chunks) for l in range(ffn1_norm.shape[0])]
    return (_trunk(x_prompt, meta_tokens, layers, chunks), _trunk(x_sample, meta_tokens, layers, chunks))
```

```python
import functools
import math

import numpy as np
import jax
import jax.numpy as jnp
from jax import lax
from jax.experimental import pallas as pl
from jax.experimental.pallas import tpu as pltpu

N_META = 16
GRID_W = 64
EPS = 1e-6
ROPE_THETA = 10000.0
MLA_HEADS = 8
MLA_NOPE = 64
MLA_ROPE = 32
MLA_QK = MLA_NOPE + MLA_ROPE
MLA_V = 64
Q_LORA = 384
KV_LORA = 256
GQA_HEADS = 8
GQA_KV_HEADS = 2
GQA_GROUP = GQA_HEADS // GQA_KV_HEADS
GQA_HD = 64
AXIAL_HALF = GQA_HD // 2
ROPE_PAIR = 16
HEAD_V = 64

LANE = 128
MXU_DIM = 256
BF16_ROWS = 16
VMEM_LIMIT = 56 * 1024 * 1024

META_PAD = LANE
VT_ROWS = HEAD_V + BF16_ROWS
KV_BLOCK = 512
ATTN_UNROLL = 4
MAX_RISE = 64.0
GQA_STREAMS = 1
ROW_TILE = 1024
LOG2E = math.log2(math.e)
NEG_BIG = -1e30

BF16 = jnp.bfloat16
F32 = jnp.float32


def _div_tile(n, target, mult):
    best = None
    for t in range(mult, min(n, target) + 1, mult):
        if n % t == 0:
            best = t
    assert best is not None, (n, target, mult)
    return best


def _ff_chunks(d_ff):
    chunks, c0 = [], 0
    while c0 < d_ff:
        w = min(3 * MXU_DIM, d_ff - c0)
        chunks.append((c0, w))
        c0 += w
    return tuple(chunks)


def _dot(a, b):
    return jnp.dot(a, b, preferred_element_type=F32)


def _rms(x, g):
    return x * lax.rsqrt(jnp.mean(x * x, axis=-1, keepdims=True) + EPS) * g


def _const_spec(shape):
    nd = len(shape)
    return pl.BlockSpec(shape, lambda *_: (0,) * nd, pipeline_mode=pl.Buffered(1))


def _params(*sem):
    return pltpu.CompilerParams(dimension_semantics=sem, vmem_limit_bytes=VMEM_LIMIT)


def _ffn_kernel(h_ref, g_ref, w13_ref, w2_ref, o_ref, *, chunks):
    h = h_ref[...]
    xn = _rms(h, g_ref[...]).astype(BF16)
    acc = None
    off = 0
    for c0, w in chunks:
        gu = _dot(xn, w13_ref[:, off:off + 2 * w])
        gate, up = gu[:, :w], gu[:, w:]
        a = (gate * jax.nn.sigmoid(gate) * up).astype(BF16)
        part = _dot(a, w2_ref[c0:c0 + w, :])
        acc = part if acc is None else acc + part
        off += 2 * w
    o_ref[...] = h + 0.5 * acc


def _ffn(h, g, w13r, w2, chunks, real_only=None):
    rows, d = h.shape
    if real_only is None:
        tm = _div_tile(rows, ROW_TILE, LANE)
        grid, sem = (rows // tm,), ("parallel",)
        spec = pl.BlockSpec((tm, d), lambda i: (i, 0))
        out_shape = jax.ShapeDtypeStruct((rows, d), F32)
    else:
        batch, seq_rows, n_real = real_only
        tm = _div_tile(n_real, ROW_TILE, LANE)
        h = h.reshape(batch, seq_rows, d)
        grid, sem = (batch, n_real // tm), ("parallel", "parallel")
        spec = pl.BlockSpec((None, tm, d), lambda b, i: (b, i, 0))
        out_shape = jax.ShapeDtypeStruct((batch, n_real, d), F32)
    return pl.pallas_call(
        functools.partial(_ffn_kernel, chunks=chunks),
        grid=grid,
        in_specs=[spec, _const_spec(g.shape), _const_spec(w13r.shape), _const_spec(w2.shape)],
        out_specs=spec,
        out_shape=out_shape,
        compiler_params=_params(*sem),
        name="ffn",
    )(h, g, w13r, w2)


_C_QG = 0
_C_KR = 512
_C_KG = 640
_C_VG = 768
_C_CQ = 896
_C_CKV = 1280
_C_END = 1536


def _group_ms(x, bd):
    return _dot((x * x).astype(BF16), bd)


def _rope(y, cos, sin, swap):
    hi = y.astype(BF16)
    lo = (y - hi.astype(F32)).astype(BF16)
    return y * cos + (_dot(hi, swap) + _dot(lo, swap)) * sin


def _store_vt(vt_ref, head0, v_pair, ones_tile):
    vt = v_pair.T
    for k in range(2):
        vt_ref[head0 + k, :HEAD_V, :] = vt[k * HEAD_V:(k + 1) * HEAD_V].astype(BF16)
        vt_ref[head0 + k, HEAD_V:, :] = ones_tile


def _proj_kernel(h_ref, gmix_ref, win_ref, gq_ref, gkv_ref, wuq_ref, wukv_ref,
                 bd_a_ref, bd_c_ref, bd_m_ref, swap_ref, gaq_ref, gak_ref, gakr_ref, gbq_ref, gbk_ref,
                 cos_a_ref, sin_a_ref, cos_b_ref, sin_b_ref,
                 qa_ref, ka_ref, vta_ref, qb_ref, kb_ref, vtb_ref):
    tm = h_ref.shape[0]
    hn = _rms(h_ref[...], gmix_ref[...]).astype(BF16)
    z = _dot(hn, win_ref[...])
    cq = _rms(z[:, _C_CQ:_C_CKV], gq_ref[...]).astype(BF16)
    ckv = _rms(z[:, _C_CKV:_C_END], gkv_ref[...]).astype(BF16)
    qa = _dot(cq, wuq_ref[...])
    kva = _dot(ckv, wukv_ref[...])
    n_kn = MLA_HEADS * MLA_NOPE

    low_half = lax.broadcasted_iota(jnp.int32, (tm, LANE), 1) < HEAD_V
    cos_a, sin_a = cos_a_ref[...], sin_a_ref[...]
    cos_b, sin_b = cos_b_ref[...], sin_b_ref[...]
    pair = lambda left, right: jnp.concatenate([left, right], axis=1)
    bd_a, bd_c, bd_m, swap = bd_a_ref[...], bd_c_ref[...], bd_m_ref[...], swap_ref[...]
    ones_tile = (lax.broadcasted_iota(jnp.int32, (BF16_ROWS, tm), 0) == 0).astype(F32).astype(BF16)

    x = z[:, _C_KR:_C_VG]
    y = x * lax.rsqrt(_group_ms(x, bd_m) + EPS) * pair(gakr_ref[...], gbk_ref[...])
    y = _rope(y, pair(cos_a, cos_b), pair(sin_a, sin_b), swap)
    kr = y[:, :LANE]
    kb_ref[...] = y[:, LANE:].astype(BF16)
    _store_vt(vtb_ref, 0, z[:, _C_VG:_C_CQ], ones_tile)
    for c in range(MLA_HEADS // 2):
        v0 = n_kn + c * LANE
        _store_vt(vta_ref, 2 * c, kva[:, v0:v0 + LANE], ones_tile)

    cos_aa, sin_aa = pair(cos_a, cos_a), pair(sin_a, sin_a)
    for c in range(MLA_HEADS * LANE // MXU_DIM):
        sl = slice(c * MXU_DIM, (c + 1) * MXU_DIM)
        x = qa[:, sl]
        y = _rope(x * lax.rsqrt(_group_ms(x, bd_a) + EPS) * gaq_ref[:, sl], cos_aa, sin_aa, swap)
        for half in range(2):
            qa_ref[2 * c + half] = y[:, half * LANE:(half + 1) * LANE].astype(BF16)

    for c in range(n_kn // MXU_DIM):
        sl = slice(c * MXU_DIM, (c + 1) * MXU_DIM)
        xk = kva[:, sl]
        yk = xk * lax.rsqrt(_group_ms(xk, bd_c) + EPS) * gak_ref[:, sl]
        for half in range(2):
            t = yk[:, half * LANE:(half + 1) * LANE]
            for k, tile in enumerate((t, pltpu.roll(t, HEAD_V, 1))):
                ka_ref[4 * c + 2 * half + k] = (jnp.where(low_half, tile, 0.0) + kr).astype(BF16)

    cos_bb, sin_bb = pair(cos_b, cos_b), pair(sin_b, sin_b)
    for kv in range(GQA_KV_HEADS):
        sl = slice(kv * MXU_DIM, (kv + 1) * MXU_DIM)
        xg = z[:, sl]
        yg = _rope(xg * lax.rsqrt(_group_ms(xg, bd_c) + EPS) * gbq_ref[:, sl], cos_bb, sin_bb, swap)
        keep = low_half if kv == 0 else jnp.logical_not(low_half)
        for half in range(2):
            t = yg[:, half * LANE:(half + 1) * LANE]
            tiles = (t, pltpu.roll(t, GQA_HD, 1)) if kv == 0 else (pltpu.roll(t, GQA_HD, 1), t)
            for k in range(2):
                head = GQA_GROUP * kv + 2 * half + k
                qb_ref[head] = jnp.where(keep, tiles[k], 0.0).astype(BF16)


def _proj(h, lw, tabs, seq_rows):
    rows, d = h.shape
    tm = _div_tile(seq_rows, 640, LANE)
    per_seq = seq_rows // tm
    row_spec = lambda w: pl.BlockSpec((tm, w), lambda i: (i, 0))
    tab_spec = pl.BlockSpec((tm, LANE), lambda i: (i % per_seq, 0))
    head_spec = pl.BlockSpec((MLA_HEADS, tm, LANE), lambda i: (0, i, 0))
    vt_spec = lambda n: pl.BlockSpec((n, VT_ROWS, tm), lambda i: (0, 0, i))
    consts = [lw["g_mix"], lw["w_in_s"], lw["g_q"], lw["g_kv"], lw["w_uq_p"], lw["w_ukv_p"],
              tabs["bd_a"], tabs["bd_c"], tabs["bd_m"], tabs["swap"],
              lw["ga_q"], lw["ga_k"], lw["ga_kr"], lw["gb_q"], lw["gb_k"]]
    heads = jax.ShapeDtypeStruct((MLA_HEADS, rows, LANE), BF16)
    return pl.pallas_call(
        _proj_kernel,
        grid=(rows // tm,),
        in_specs=[row_spec(d)] + [_const_spec(c.shape) for c in consts] + [tab_spec] * 4,
        out_specs=[head_spec, head_spec, vt_spec(MLA_HEADS), head_spec, row_spec(LANE),
                   vt_spec(GQA_KV_HEADS)],
        out_shape=[heads, heads, jax.ShapeDtypeStruct((MLA_HEADS, VT_ROWS, rows), BF16), heads,
                   jax.ShapeDtypeStruct((rows, LANE), BF16),
                   jax.ShapeDtypeStruct((GQA_KV_HEADS, VT_ROWS, rows), BF16)],
        compiler_params=_params("parallel"),
        name="proj",
    )(h, *consts, tabs["cos_a"], tabs["sin_a"], tabs["cos_b"], tabs["sin_b"])


def _lane_ds(start, size):
    return pl.ds(start if isinstance(start, int) else pl.multiple_of(start, LANE), size)


def _col_max(s):
    return jnp.max(s, axis=0, keepdims=True)


def _flash_update(s, s_max, vt, carry):
    m, acc = carry
    m_new = jnp.maximum(m, s_max)
    p = jnp.exp2(s - m_new).astype(BF16)
    acc = jnp.exp2(m - m_new) * acc + _dot(vt, p)
    return m_new, acc


def _scores_fn(qs, k_at):
    def scores(i, start, size):
        return lax.dot_general(k_at(i, start, size), qs[i], (((1,), (1,)), ((), ())),
                               preferred_element_type=F32)
    return scores


def _looped_blocks(n_blocks, unroll):
    return n_blocks // unroll * unroll if n_blocks >= 2 * unroll else 0


def _meta_mask(nq):
    return lax.broadcasted_iota(jnp.int32, (META_PAD, nq), 0) < N_META


def _flash_one_pass(qs, k_at, vt_at, n_real, tk):
    n = len(qs)
    n_blk = n_real // tk
    assert n_real % tk == 0
    scores = _scores_fn(qs, k_at)

    def fold(i, start, size, carry, mask=None):
        m, acc, rise = carry
        s = scores(i, start, size)
        if mask is not None:
            s = jnp.where(mask, s, NEG_BIG)
        s_max = _col_max(s)
        acc = acc + _dot(vt_at(i, start, size), jnp.exp2(s - m).astype(BF16))
        m_new = jnp.maximum(m, s_max)
        return m_new, acc * jnp.exp2(m - m_new), jnp.maximum(rise, s_max - m)

    def sweep(first, count, carries):
        for k in range(count):
            carries = [fold(i, (first + k) * tk, tk, carries[i]) for i in range(n)]
        return carries

    carries = []
    for i in range(n):
        s = scores(i, 0, tk)
        m = _col_max(s)
        carries.append((m, _dot(vt_at(i, 0, tk), jnp.exp2(s - m).astype(BF16)), jnp.zeros_like(m)))
    looped = _looped_blocks(n_blk - 1, ATTN_UNROLL)
    if looped:
        carries = lax.fori_loop(0, looped // ATTN_UNROLL,
                                lambda t, c: sweep(1 + t * ATTN_UNROLL, ATTN_UNROLL, c), carries)
    carries = sweep(1 + looped, n_blk - 1 - looped, carries)
    outs, rise = [], None
    for i in range(n):
        _, acc, r = fold(i, n_real, META_PAD, carries[i], _meta_mask(qs[i].shape[0]))
        outs.append(acc[:HEAD_V] / acc[HEAD_V:HEAD_V + 1])
        rise = jnp.max(r) if rise is None else jnp.maximum(rise, jnp.max(r))
    return outs, rise


def _attend(qs, k_at, vt_at, s_refs, n_real, tk, write):
    outs, rise = _flash_one_pass(qs, k_at, vt_at, n_real, tk)
    write(outs)

    @pl.when(rise > MAX_RISE)
    def _():
        write(_flash(qs, k_at, vt_at, s_refs, n_real, tk))


def _flash(qs, k_at, vt_at, s_refs, n_real, tk):
    n = len(qs)
    n_blk = n_real // tk
    assert n_real % tk == 0
    scores = _scores_fn(qs, k_at)

    def issue(blk, s_buf):
        maxes = []
        for i in range(n):
            s = scores(i, blk * tk, tk)
            s_buf[i] = s
            maxes.append(_col_max(s))
        return maxes

    def stage(blk, parity, state, issue_next=True):
        carries, maxes = state
        next_maxes = issue(blk + 1, s_refs[1 - parity]) if issue_next else maxes
        carries = [_flash_update(s_refs[parity][i], maxes[i], vt_at(i, blk * tk, tk), carries[i])
                   for i in range(n)]
        return carries, next_maxes

    carries = [(jnp.full((1, q.shape[0]), NEG_BIG, F32), jnp.zeros((VT_ROWS, q.shape[0]), F32)) for q in qs]
    state = (carries, issue(0, s_refs[0]))
    looped = _looped_blocks(n_blk - 1, 2)
    if looped:
        def body(t, state):
            return stage(2 * t + 1, 1, stage(2 * t, 0, state))
        state = lax.fori_loop(0, looped // 2, body, state)
    for blk in range(looped, n_blk):
        state = stage(blk, blk % 2, state, issue_next=blk + 1 < n_blk)
    carries, _ = state
    outs = []
    for i in range(n):
        s_meta = jnp.where(_meta_mask(qs[i].shape[0]), scores(i, n_real, META_PAD), NEG_BIG)
        _, acc = _flash_update(s_meta, _col_max(s_meta), vt_at(i, n_real, META_PAD), carries[i])
        outs.append(acc[:HEAD_V] / acc[HEAD_V:HEAD_V + 1])
    return outs


def _mla_kernel(q_ref, k_ref, vt_ref, o_ref, s0_ref, s1_ref, *, n_real, tk):
    qs = [q_ref[0, 0], q_ref[1, 0]]
    k_at = lambda i, s, n: k_ref[i, 0, _lane_ds(s, n), :]
    vt_at = lambda i, s, n: vt_ref[i, :, _lane_ds(s, n)]

    def write(outs):
        o_ref[0] = jnp.concatenate(outs, axis=0).T.astype(BF16)

    _attend(qs, k_at, vt_at, (s0_ref, s1_ref), n_real, tk, write)


def _kv_tile(n_real):
    return _div_tile(n_real, KV_BLOCK, LANE)


def _mla_attn(qa, ka, vta, batch, seq_rows):
    n_real = seq_rows - META_PAD
    tq = _div_tile(seq_rows, 1792, LANE)
    tk = _kv_tile(n_real)
    q4 = qa.reshape(MLA_HEADS, batch, seq_rows, LANE)
    k4 = ka.reshape(MLA_HEADS, batch, seq_rows, LANE)
    return pl.pallas_call(
        functools.partial(_mla_kernel, n_real=n_real, tk=tk),
        grid=(batch, MLA_HEADS // 2, seq_rows // tq),
        in_specs=[
            pl.BlockSpec((2, 1, tq, LANE), lambda b, j, i: (j, b, i, 0)),
            pl.BlockSpec((2, 1, seq_rows, LANE), lambda b, j, i: (j, b, 0, 0)),
            pl.BlockSpec((2, VT_ROWS, seq_rows), lambda b, j, i: (j, 0, b)),
        ],
        out_specs=pl.BlockSpec((1, tq, LANE), lambda b, j, i: (b, i, j)),
        out_shape=jax.ShapeDtypeStruct((batch, seq_rows, MLA_HEADS * MLA_V), BF16),
        scratch_shapes=[pltpu.VMEM((2, tk, tq), F32), pltpu.VMEM((2, tk, tq), F32)],
        compiler_params=_params("parallel", "parallel", "parallel"),
        name="mla_attn",
    )(q4, k4, vta).reshape(batch * seq_rows, MLA_HEADS * MLA_V)


def _gqa_kernel(q_ref, k_ref, vt_ref, o_ref, s0_ref, s1_ref, *, n_real, tk):
    g, _, tq, _ = q_ref.shape
    per = g // GQA_STREAMS
    qs = [q_ref[i * per:(i + 1) * per, 0].reshape(per * tq, LANE) for i in range(GQA_STREAMS)]
    k_at = lambda i, s, n: k_ref[0, _lane_ds(s, n), :]
    vt_at = lambda i, s, n: vt_ref[0, :, _lane_ds(s, n)]

    def write(outs):
        heads = [o[:, i * tq:(i + 1) * tq] for o in outs for i in range(per)]
        o_ref[0] = jnp.concatenate(heads, axis=0).T.astype(BF16)

    _attend(qs, k_at, vt_at, (s0_ref, s1_ref), n_real, tk, write)


def _gqa_attn(qb, kb, vtb, batch, seq_rows):
    n_real = seq_rows - META_PAD
    per = GQA_GROUP // GQA_STREAMS
    tq = _div_tile(seq_rows, 704, LANE // per)
    tk = _kv_tile(n_real)
    q4 = qb.reshape(GQA_HEADS, batch, seq_rows, LANE)
    k3 = kb.reshape(batch, seq_rows, LANE)
    width = GQA_GROUP * GQA_HD
    s_buf = pltpu.VMEM((GQA_STREAMS, tk, per * tq), F32)
    return pl.pallas_call(
        functools.partial(_gqa_kernel, n_real=n_real, tk=tk),
        grid=(batch, GQA_KV_HEADS, seq_rows // tq),
        in_specs=[
            pl.BlockSpec((GQA_GROUP, 1, tq, LANE), lambda b, j, i: (j, b, i, 0)),
            pl.BlockSpec((1, seq_rows, LANE), lambda b, j, i: (b, 0, 0)),
            pl.BlockSpec((1, VT_ROWS, seq_rows), lambda b, j, i: (j, 0, b)),
        ],
        out_specs=pl.BlockSpec((1, tq, width), lambda b, j, i: (b, i, j)),
        out_shape=jax.ShapeDtypeStruct((batch, seq_rows, GQA_HEADS * GQA_HD), BF16),
        scratch_shapes=[s_buf, s_buf],
        compiler_params=_params("parallel", "parallel", "parallel"),
        name="gqa_attn",
    )(q4, k3, vtb).reshape(batch * seq_rows, GQA_HEADS * GQA_HD)


def _merge_kernel(h_ref, oa_ref, ob_ref, gmix_ref, wg_ref, bg_ref, wao_ref, wbo_ref, wout_ref, o_ref):
    h = h_ref[...]
    d = h.shape[-1]
    hn = _rms(h, gmix_ref[...]).astype(BF16)
    gates = jax.nn.sigmoid(_dot(hn, wg_ref[...]) + bg_ref[...])
    merged = gates[:, :d] * _dot(oa_ref[...], wao_ref[...]) + gates[:, d:] * _dot(ob_ref[...], wbo_ref[...])
    o_ref[...] = h + _dot(merged.astype(BF16), wout_ref[...])


def _merge(h, oa, ob, lw):
    rows, d = h.shape
    tm = _div_tile(rows, ROW_TILE, LANE)
    row_spec = lambda w: pl.BlockSpec((tm, w), lambda i: (i, 0))
    consts = [lw["g_mix"], lw["w_gates"], lw["b_gate"], lw["w_mla_o"], lw["w_gqa_o"], lw["w_out"]]
    return pl.pallas_call(
        _merge_kernel,
        grid=(rows // tm,),
        in_specs=[row_spec(d), row_spec(oa.shape[1]), row_spec(ob.shape[1])]
        + [_const_spec(c.shape) for c in consts],
        out_specs=row_spec(d),
        out_shape=jax.ShapeDtypeStruct((rows, d), F32),
        compiler_params=_params("parallel"),
        name="merge",
    )(h, oa, ob, *consts)


def _block_diag_mean(groups):
    assert sum(groups) == MXU_DIM
    m = np.zeros((MXU_DIM, MXU_DIM), np.float32)
    c0 = 0
    for gsz in groups:
        m[c0:c0 + gsz, c0:c0 + gsz] = 1.0 / gsz
        c0 += gsz
    return jnp.asarray(m, BF16)


def _rotate_half_swap():
    lanes = np.arange(MXU_DIM)
    src = np.where(lanes % (2 * ROPE_PAIR) < ROPE_PAIR, lanes + ROPE_PAIR, lanes - ROPE_PAIR)
    m = np.zeros((MXU_DIM, MXU_DIM), np.float32)
    m[src, lanes] = 1.0
    return jnp.asarray(m, BF16)


def _rope_angles(pos, dim):
    inv = 1.0 / (ROPE_THETA ** (jnp.arange(0, dim, 2, dtype=jnp.float32) / dim))
    return pos.astype(jnp.float32)[:, None] * inv[None, :]


def _tables(n_real):
    p = jnp.arange(n_real + META_PAD, dtype=jnp.int32)
    real = p < n_real
    meta = (p >= n_real) & (p < n_real + N_META)
    pos_1d = jnp.where(real, p + N_META, jnp.where(meta, p - n_real, 0))
    row = jnp.where(real, p // GRID_W, 0)
    col = jnp.where(real, p % GRID_W, 0)
    a1, ar, ac = _rope_angles(pos_1d, MLA_ROPE), _rope_angles(row, AXIAL_HALF), _rope_angles(col, AXIAL_HALF)
    ones = jnp.ones((p.shape[0], MLA_NOPE), F32)
    pad1 = jnp.ones((p.shape[0], LANE - MLA_QK), F32)
    cos_a = jnp.concatenate([ones, jnp.cos(a1), jnp.cos(a1), pad1], axis=1)
    sin_a = jnp.concatenate([0 * ones, -jnp.sin(a1), jnp.sin(a1), 0 * pad1], axis=1)
    cos_b = jnp.tile(jnp.concatenate([jnp.cos(ar), jnp.cos(ar), jnp.cos(ac), jnp.cos(ac)], axis=1), (1, 2))
    sin_b = jnp.tile(jnp.concatenate([-jnp.sin(ar), jnp.sin(ar), -jnp.sin(ac), jnp.sin(ac)], axis=1), (1, 2))
    return dict(
        cos_a=cos_a, sin_a=sin_a, cos_b=cos_b, sin_b=sin_b,
        bd_a=_block_diag_mean([MLA_NOPE, MLA_ROPE, LANE - MLA_QK] * 2),
        bd_c=_block_diag_mean([GQA_HD] * 4),
        bd_m=_block_diag_mean([MLA_NOPE, MLA_ROPE, LANE - MLA_QK, GQA_HD, GQA_HD]),
        swap=_rotate_half_swap(),
    )


def _prep_layer(ffn1_norm, ffn1_w13, ffn1_w2, mix_norm, w_in, b_gate, q_a_norm, kv_a_norm, w_uq, w_ukv,
                mla_q_norm, mla_k_norm, gqa_q_norm, gqa_k_norm, w_mla_o, w_gqa_o, w_out,
                ffn2_norm, ffn2_w13, ffn2_w2, chunks):
    d = w_in.shape[0]
    d_ff = ffn1_w2.shape[0]

    def w13r(w13):
        pieces = []
        for c0, w in chunks:
            pieces += [w13[:, c0:c0 + w], w13[:, d_ff + c0:d_ff + c0 + w]]
        return jnp.concatenate(pieces, axis=1).astype(BF16)

    cuts = np.cumsum([0, Q_LORA, KV_LORA, MLA_ROPE, GQA_HEADS * GQA_HD, GQA_KV_HEADS * GQA_HD,
                      GQA_KV_HEADS * GQA_HD]).tolist()
    w_cq, w_ckv, w_kr, w_qg, w_kg, w_vg = (w_in[:, cuts[i]:cuts[i + 1]] for i in range(6))
    w_gates = w_in[:, cuts[6]:]
    kr_cols = [jnp.zeros((d, MLA_NOPE), F32), w_kr, jnp.zeros((d, LANE - MLA_QK), F32)]
    w_in_s = jnp.concatenate([w_qg] + kr_cols + [w_kg, w_vg, w_cq, w_ckv], axis=1).astype(BF16)
    assert w_in_s.shape[1] == _C_END

    uq_cols, ukv_cols, v_cols = [], [], []
    for hd in range(MLA_HEADS):
        uq_cols += [w_uq[:, hd * MLA_QK:(hd + 1) * MLA_QK], jnp.zeros((Q_LORA, LANE - MLA_QK), F32)]
        base = hd * (MLA_NOPE + MLA_V)
        ukv_cols += [w_ukv[:, base:base + MLA_NOPE]]
        v_cols += [w_ukv[:, base + MLA_NOPE:base + MLA_NOPE + MLA_V]]
    w_uq_p = jnp.concatenate(uq_cols, axis=1).astype(BF16)
    w_ukv_p = jnp.concatenate(ukv_cols + v_cols, axis=1).astype(BF16)

    zq = jnp.zeros((LANE - MLA_QK,), F32)
    ga_q = jnp.tile(jnp.concatenate([mla_q_norm, zq]), MLA_HEADS) * (MLA_QK ** -0.5 * LOG2E)
    ga_k = jnp.tile(mla_k_norm[:MLA_NOPE], MLA_HEADS)
    ga_kr = jnp.concatenate([jnp.zeros((MLA_NOPE,), F32), mla_k_norm[MLA_NOPE:], zq])
    gb_q = jnp.tile(gqa_q_norm, GQA_HEADS) * (GQA_HD ** -0.5 * LOG2E)
    gb_k = jnp.tile(gqa_k_norm, GQA_KV_HEADS)
    row = lambda v: v.reshape(1, -1).astype(F32)
    return dict(
        g_ffn1=row(ffn1_norm), w13r_1=w13r(ffn1_w13), w2_1=ffn1_w2.astype(BF16),
        g_ffn2=row(ffn2_norm), w13r_2=w13r(ffn2_w13), w2_2=ffn2_w2.astype(BF16),
        g_mix=row(mix_norm), w_in_s=w_in_s, g_q=row(q_a_norm), g_kv=row(kv_a_norm),
        w_uq_p=w_uq_p, w_ukv_p=w_ukv_p,
        ga_q=row(ga_q), ga_k=row(ga_k), ga_kr=row(ga_kr), gb_q=row(gb_q), gb_k=row(gb_k),
        w_gates=w_gates.astype(BF16), b_gate=row(b_gate),
        w_mla_o=w_mla_o.astype(BF16), w_gqa_o=w_gqa_o.astype(BF16), w_out=w_out.astype(BF16),
    )


def _trunk(x, meta_tokens, layers, chunks):
    batch, n_real, d = x.shape
    assert n_real % GRID_W == 0 and n_real % (8 * LANE) == 0
    seq_rows = n_real + META_PAD
    tail = jnp.concatenate([meta_tokens.astype(x.dtype), jnp.zeros((META_PAD - N_META, d), x.dtype)], axis=0)
    h = jnp.concatenate([x, jnp.broadcast_to(tail[None], (batch, META_PAD, d))], axis=1)
    h = h.reshape(batch * seq_rows, d)
    tabs = _tables(n_real)
    for idx, lw in enumerate(layers):
        h = _ffn(h, lw["g_ffn1"], lw["w13r_1"], lw["w2_1"], chunks)
        qa, ka, vta, qb, kb, vtb = _proj(h, lw, tabs, seq_rows)
        oa = _mla_attn(qa, ka, vta, batch, seq_rows)
        ob = _gqa_attn(qb, kb, vtb, batch, seq_rows)
        h = _merge(h, oa, ob, lw)
        real_only = (batch, seq_rows, n_real) if idx == len(layers) - 1 else None
        h = _ffn(h, lw["g_ffn2"], lw["w13r_2"], lw["w2_2"], chunks, real_only)
    return h


def kernel(x_prompt, x_sample, meta_tokens, ffn1_norm, ffn1_w13, ffn1_w2, mix_norm, w_in, b_gate, q_a_norm,
           kv_a_norm, w_uq, w_ukv, mla_q_norm, mla_k_norm, gqa_q_norm, gqa_k_norm, w_mla_o, w_gqa_o, w_out,
           ffn2_norm, ffn2_w13, ffn2_w2):
    per_layer = (ffn1_norm, ffn1_w13, ffn1_w2, mix_norm, w_in, b_gate, q_a_norm, kv_a_norm, w_uq, w_ukv,
                 mla_q_norm, mla_k_norm, gqa_q_norm, gqa_k_norm, w_mla_o, w_gqa_o, w_out,
                 ffn2_norm, ffn2_w13, ffn2_w2)
    chunks = _ff_chunks(ffn1_w2.shape[1])
    layers = [_prep_layer(*(p[l] for p in per_layer), chunks=chunks) for l in range(ffn1_norm.shape[0])]
    return (_trunk(x_prompt, meta_tokens, layers, chunks), _trunk(x_sample, meta_tokens, layers, chunks))
```

```python
import functools
import math

import numpy as np
import jax
import jax.numpy as jnp
from jax import lax
from jax.experimental import pallas as pl
from jax.experimental.pallas import tpu as pltpu

N_META = 16
GRID_W = 64
EPS = 1e-6
ROPE_THETA = 10000.0
MLA_HEADS = 8
MLA_NOPE = 64
MLA_ROPE = 32
MLA_QK = MLA_NOPE + MLA_ROPE
MLA_V = 64
Q_LORA = 384
KV_LORA = 256
GQA_HEADS = 8
GQA_KV_HEADS = 2
GQA_GROUP = GQA_HEADS // GQA_KV_HEADS
GQA_HD = 64
AXIAL_HALF = GQA_HD // 2
ROPE_PAIR = 16
HEAD_V = 64

LANE = 128
MXU_DIM = 256
BF16_ROWS = 16
VMEM_LIMIT = 56 * 1024 * 1024

META_PAD = LANE
VT_ROWS = HEAD_V + BF16_ROWS
KV_BLOCK = 512
ATTN_UNROLL = 4
MAX_RISE = 24.0
ROW_TILE = 1024
LOG2E = math.log2(math.e)
NEG_BIG = -1e30

BF16 = jnp.bfloat16
F32 = jnp.float32


def _div_tile(n, target, mult):
    best = None
    for t in range(mult, min(n, target) + 1, mult):
        if n % t == 0:
            best = t
    assert best is not None, (n, target, mult)
    return best


def _ff_chunks(d_ff):
    chunks, c0 = [], 0
    while c0 < d_ff:
        w = min(3 * MXU_DIM, d_ff - c0)
        chunks.append((c0, w))
        c0 += w
    return tuple(chunks)


def _dot(a, b):
    return jnp.dot(a, b, preferred_element_type=F32)


def _rms(x, g):
    return x * lax.rsqrt(jnp.mean(x * x, axis=-1, keepdims=True) + EPS) * g


def _const_spec(shape):
    nd = len(shape)
    return pl.BlockSpec(shape, lambda *_: (0,) * nd, pipeline_mode=pl.Buffered(1))


def _params(*sem):
    return pltpu.CompilerParams(dimension_semantics=sem, vmem_limit_bytes=VMEM_LIMIT)


def _ffn_kernel(h_ref, g_ref, w13_ref, w2_ref, o_ref, *, chunks):
    h = h_ref[...]
    xn = _rms(h, g_ref[...]).astype(BF16)
    acc = None
    off = 0
    for c0, w in chunks:
        gu = _dot(xn, w13_ref[:, off:off + 2 * w])
        gate, up = gu[:, :w], gu[:, w:]
        a = (gate * jax.nn.sigmoid(gate) * up).astype(BF16)
        part = _dot(a, w2_ref[c0:c0 + w, :])
        acc = part if acc is None else acc + part
        off += 2 * w
    o_ref[...] = h + 0.5 * acc


def _ffn(h, g, w13r, w2, chunks, real_only=None):
    rows, d = h.shape
    if real_only is None:
        tm = _div_tile(rows, ROW_TILE, LANE)
        grid, sem = (rows // tm,), ("parallel",)
        spec = pl.BlockSpec((tm, d), lambda i: (i, 0))
        out_shape = jax.ShapeDtypeStruct((rows, d), F32)
    else:
        batch, seq_rows, n_real = real_only
        tm = _div_tile(n_real, ROW_TILE, LANE)
        h = h.reshape(batch, seq_rows, d)
        grid, sem = (batch, n_real // tm), ("parallel", "parallel")
        spec = pl.BlockSpec((None, tm, d), lambda b, i: (b, i, 0))
        out_shape = jax.ShapeDtypeStruct((batch, n_real, d), F32)
    return pl.pallas_call(
        functools.partial(_ffn_kernel, chunks=chunks),
        grid=grid,
        in_specs=[spec, _const_spec(g.shape), _const_spec(w13r.shape), _const_spec(w2.shape)],
        out_specs=spec,
        out_shape=out_shape,
        compiler_params=_params(*sem),
        name="ffn",
    )(h, g, w13r, w2)


_C_QG = 0
_C_KR = 512
_C_KG = 640
_C_VG = 768
_C_CQ = 896
_C_CKV = 1280
_C_END = 1536


def _group_ms(x, bd):
    return _dot((x * x).astype(BF16), bd)


def _rope(y, cos, sin, swap):
    hi = y.astype(BF16)
    lo = (y - hi.astype(F32)).astype(BF16)
    return y * cos + (_dot(hi, swap) + _dot(lo, swap)) * sin


def _store_vt(vt_ref, head0, v_pair, ones_tile):
    vt = v_pair.T
    for k in range(2):
        vt_ref[head0 + k, :HEAD_V, :] = vt[k * HEAD_V:(k + 1) * HEAD_V].astype(BF16)
        vt_ref[head0 + k, HEAD_V:, :] = ones_tile


def _proj_kernel(h_ref, gmix_ref, win_ref, gq_ref, gkv_ref, wuq_ref, wukv_ref,
                 bd_a_ref, bd_c_ref, bd_m_ref, swap_ref, gaq_ref, gak_ref, gakr_ref, gbq_ref, gbk_ref,
                 cos_a_ref, sin_a_ref, cos_b_ref, sin_b_ref,
                 qa_ref, ka_ref, vta_ref, qb_ref, kb_ref, vtb_ref):
    tm = h_ref.shape[0]
    hn = _rms(h_ref[...], gmix_ref[...]).astype(BF16)
    z = _dot(hn, win_ref[...])
    cq = _rms(z[:, _C_CQ:_C_CKV], gq_ref[...]).astype(BF16)
    ckv = _rms(z[:, _C_CKV:_C_END], gkv_ref[...]).astype(BF16)
    qa = _dot(cq, wuq_ref[...])
    kva = _dot(ckv, wukv_ref[...])
    n_kn = MLA_HEADS * MLA_NOPE

    low_half = lax.broadcasted_iota(jnp.int32, (tm, LANE), 1) < HEAD_V
    cos_a, sin_a = cos_a_ref[...], sin_a_ref[...]
    cos_b, sin_b = cos_b_ref[...], sin_b_ref[...]
    pair = lambda left, right: jnp.concatenate([left, right], axis=1)
    bd_a, bd_c, bd_m, swap = bd_a_ref[...], bd_c_ref[...], bd_m_ref[...], swap_ref[...]
    ones_tile = (lax.broadcasted_iota(jnp.int32, (BF16_ROWS, tm), 0) == 0).astype(F32).astype(BF16)

    x = z[:, _C_KR:_C_VG]
    y = x * lax.rsqrt(_group_ms(x, bd_m) + EPS) * pair(gakr_ref[...], gbk_ref[...])
    y = _rope(y, pair(cos_a, cos_b), pair(sin_a, sin_b), swap)
    kr = y[:, :LANE]
    kb_ref[...] = y[:, LANE:].astype(BF16)
    _store_vt(vtb_ref, 0, z[:, _C_VG:_C_CQ], ones_tile)
    for c in range(MLA_HEADS // 2):
        v0 = n_kn + c * LANE
        _store_vt(vta_ref, 2 * c, kva[:, v0:v0 + LANE], ones_tile)

    cos_aa, sin_aa = pair(cos_a, cos_a), pair(sin_a, sin_a)
    for c in range(MLA_HEADS * LANE // MXU_DIM):
        sl = slice(c * MXU_DIM, (c + 1) * MXU_DIM)
        x = qa[:, sl]
        y = _rope(x * lax.rsqrt(_group_ms(x, bd_a) + EPS) * gaq_ref[:, sl], cos_aa, sin_aa, swap)
        for half in range(2):
            qa_ref[2 * c + half] = y[:, half * LANE:(half + 1) * LANE].astype(BF16)

    for c in range(n_kn // MXU_DIM):
        sl = slice(c * MXU_DIM, (c + 1) * MXU_DIM)
        xk = kva[:, sl]
        yk = xk * lax.rsqrt(_group_ms(xk, bd_c) + EPS) * gak_ref[:, sl]
        for half in range(2):
            t = yk[:, half * LANE:(half + 1) * LANE]
            for k, tile in enumerate((t, pltpu.roll(t, HEAD_V, 1))):
                ka_ref[4 * c + 2 * half + k] = (jnp.where(low_half, tile, 0.0) + kr).astype(BF16)

    cos_bb, sin_bb = pair(cos_b, cos_b), pair(sin_b, sin_b)
    for kv in range(GQA_KV_HEADS):
        sl = slice(kv * MXU_DIM, (kv + 1) * MXU_DIM)
        xg = z[:, sl]
        yg = _rope(xg * lax.rsqrt(_group_ms(xg, bd_c) + EPS) * gbq_ref[:, sl], cos_bb, sin_bb, swap)
        keep = low_half if kv == 0 else jnp.logical_not(low_half)
        for half in range(2):
            t = yg[:, half * LANE:(half + 1) * LANE]
            tiles = (t, pltpu.roll(t, GQA_HD, 1)) if kv == 0 else (pltpu.roll(t, GQA_HD, 1), t)
            for k in range(2):
                head = GQA_GROUP * kv + 2 * half + k
                qb_ref[head] = jnp.where(keep, tiles[k], 0.0).astype(BF16)


def _proj(h, lw, tabs, seq_rows):
    rows, d = h.shape
    tm = _div_tile(seq_rows, 640, LANE)
    per_seq = seq_rows // tm
    row_spec = lambda w: pl.BlockSpec((tm, w), lambda i: (i, 0))
    tab_spec = pl.BlockSpec((tm, LANE), lambda i: (i % per_seq, 0))
    head_spec = pl.BlockSpec((MLA_HEADS, tm, LANE), lambda i: (0, i, 0))
    vt_spec = lambda n: pl.BlockSpec((n, VT_ROWS, tm), lambda i: (0, 0, i))
    consts = [lw["g_mix"], lw["w_in_s"], lw["g_q"], lw["g_kv"], lw["w_uq_p"], lw["w_ukv_p"],
              tabs["bd_a"], tabs["bd_c"], tabs["bd_m"], tabs["swap"],
              lw["ga_q"], lw["ga_k"], lw["ga_kr"], lw["gb_q"], lw["gb_k"]]
    heads = jax.ShapeDtypeStruct((MLA_HEADS, rows, LANE), BF16)
    return pl.pallas_call(
        _proj_kernel,
        grid=(rows // tm,),
        in_specs=[row_spec(d)] + [_const_spec(c.shape) for c in consts] + [tab_spec] * 4,
        out_specs=[head_spec, head_spec, vt_spec(MLA_HEADS), head_spec, row_spec(LANE),
                   vt_spec(GQA_KV_HEADS)],
        out_shape=[heads, heads, jax.ShapeDtypeStruct((MLA_HEADS, VT_ROWS, rows), BF16), heads,
                   jax.ShapeDtypeStruct((rows, LANE), BF16),
                   jax.ShapeDtypeStruct((GQA_KV_HEADS, VT_ROWS, rows), BF16)],
        compiler_params=_params("parallel"),
        name="proj",
    )(h, *consts, tabs["cos_a"], tabs["sin_a"], tabs["cos_b"], tabs["sin_b"])


def _lane_ds(start, size):
    return pl.ds(start if isinstance(start, int) else pl.multiple_of(start, LANE), size)


def _col_max(s):
    return jnp.max(s, axis=0, keepdims=True)


def _flash_update(s, s_max, vt, carry):
    m, acc = carry
    m_new = jnp.maximum(m, s_max)
    p = jnp.exp2(s - m_new).astype(BF16)
    acc = jnp.exp2(m - m_new) * acc + _dot(vt, p)
    return m_new, acc


def _scores_fn(qs, k_at):
    def scores(i, start, size):
        return lax.dot_general(k_at(i, start, size), qs[i], (((1,), (1,)), ((), ())),
                               preferred_element_type=F32)
    return scores


def _looped_blocks(n_blocks, unroll):
    return n_blocks // unroll * unroll if n_blocks >= 2 * unroll else 0


def _meta_mask(nq):
    return lax.broadcasted_iota(jnp.int32, (META_PAD, nq), 0) < N_META


def _flash_one_pass(qs, k_at, vt_at, n_real, tk):
    n = len(qs)
    n_blk = n_real // tk
    assert n_real % tk == 0
    scores = _scores_fn(qs, k_at)

    def fold(i, start, size, carry, mask=None):
        m, acc, rise = carry
        s = scores(i, start, size)
        if mask is not None:
            s = jnp.where(mask, s, NEG_BIG)
        s_max = _col_max(s)
        acc = acc + _dot(vt_at(i, start, size), jnp.exp2(s - m).astype(BF16))
        m_new = jnp.maximum(m, s_max)
        return m_new, acc * jnp.exp2(m - m_new), jnp.maximum(rise, s_max - m)

    def sweep(first, count, carries):
        for k in range(count):
            carries = [fold(i, (first + k) * tk, tk, carries[i]) for i in range(n)]
        return carries

    carries = []
    for i in range(n):
        s = scores(i, 0, tk)
        m = _col_max(s)
        carries.append((m, _dot(vt_at(i, 0, tk), jnp.exp2(s - m).astype(BF16)), jnp.zeros_like(m)))
    looped = _looped_blocks(n_blk - 1, ATTN_UNROLL)
    if looped:
        carries = lax.fori_loop(0, looped // ATTN_UNROLL,
                                lambda t, c: sweep(1 + t * ATTN_UNROLL, ATTN_UNROLL, c), carries)
    carries = sweep(1 + looped, n_blk - 1 - looped, carries)
    outs, rise = [], None
    for i in range(n):
        _, acc, r = fold(i, n_real, META_PAD, carries[i], _meta_mask(qs[i].shape[0]))
        outs.append(acc[:HEAD_V] / acc[HEAD_V:HEAD_V + 1])
        rise = jnp.max(r) if rise is None else jnp.maximum(rise, jnp.max(r))
    return outs, rise


def _attend(qs, k_at, vt_at, s_refs, n_real, tk, write):
    outs, rise = _flash_one_pass(qs, k_at, vt_at, n_real, tk)
    write(outs)

    @pl.when(rise > MAX_RISE)
    def _():
        write(_flash(qs, k_at, vt_at, s_refs, n_real, tk))


def _flash(qs, k_at, vt_at, s_refs, n_real, tk):
    n = len(qs)
    n_blk = n_real // tk
    assert n_real % tk == 0
    scores = _scores_fn(qs, k_at)

    def issue(blk, s_buf):
        maxes = []
        for i in range(n):
            s = scores(i, blk * tk, tk)
            s_buf[i] = s
            maxes.append(_col_max(s))
        return maxes

    def stage(blk, parity, state, issue_next=True):
        carries, maxes = state
        next_maxes = issue(blk + 1, s_refs[1 - parity]) if issue_next else maxes
        carries = [_flash_update(s_refs[parity][i], maxes[i], vt_at(i, blk * tk, tk), carries[i])
                   for i in range(n)]
        return carries, next_maxes

    carries = [(jnp.full((1, q.shape[0]), NEG_BIG, F32), jnp.zeros((VT_ROWS, q.shape[0]), F32)) for q in qs]
    state = (carries, issue(0, s_refs[0]))
    looped = _looped_blocks(n_blk - 1, 2)
    if looped:
        def body(t, state):
            return stage(2 * t + 1, 1, stage(2 * t, 0, state))
        state = lax.fori_loop(0, looped // 2, body, state)
    for blk in range(looped, n_blk):
        state = stage(blk, blk % 2, state, issue_next=blk + 1 < n_blk)
    carries, _ = state
    outs = []
    for i in range(n):
        s_meta = jnp.where(_meta_mask(qs[i].shape[0]), scores(i, n_real, META_PAD), NEG_BIG)
        _, acc = _flash_update(s_meta, _col_max(s_meta), vt_at(i, n_real, META_PAD), carries[i])
        outs.append(acc[:HEAD_V] / acc[HEAD_V:HEAD_V + 1])
    return outs


def _mla_kernel(q_ref, k_ref, vt_ref, o_ref, s0_ref, s1_ref, *, n_real, tk):
    qs = [q_ref[0, 0], q_ref[1, 0]]
    k_at = lambda i, s, n: k_ref[i, 0, _lane_ds(s, n), :]
    vt_at = lambda i, s, n: vt_ref[i, :, _lane_ds(s, n)]

    def write(outs):
        o_ref[0] = jnp.concatenate(outs, axis=0).T.astype(BF16)

    _attend(qs, k_at, vt_at, (s0_ref, s1_ref), n_real, tk, write)


def _kv_tile(n_real):
    return _div_tile(n_real, KV_BLOCK, LANE)


def _mla_attn(qa, ka, vta, batch, seq_rows):
    n_real = seq_rows - META_PAD
    tq = _div_tile(seq_rows, 1792, LANE)
    tk = _kv_tile(n_real)
    q4 = qa.reshape(MLA_HEADS, batch, seq_rows, LANE)
    k4 = ka.reshape(MLA_HEADS, batch, seq_rows, LANE)
    return pl.pallas_call(
        functools.partial(_mla_kernel, n_real=n_real, tk=tk),
        grid=(batch, MLA_HEADS // 2, seq_rows // tq),
        in_specs=[
            pl.BlockSpec((2, 1, tq, LANE), lambda b, j, i: (j, b, i, 0)),
            pl.BlockSpec((2, 1, seq_rows, LANE), lambda b, j, i: (j, b, 0, 0)),
            pl.BlockSpec((2, VT_ROWS, seq_rows), lambda b, j, i: (j, 0, b)),
        ],
        out_specs=pl.BlockSpec((1, tq, LANE), lambda b, j, i: (b, i, j)),
        out_shape=jax.ShapeDtypeStruct((batch, seq_rows, MLA_HEADS * MLA_V), BF16),
        scratch_shapes=[pltpu.VMEM((2, tk, tq), F32), pltpu.VMEM((2, tk, tq), F32)],
        compiler_params=_params("parallel", "parallel", "parallel"),
        name="mla_attn",
    )(q4, k4, vta).reshape(batch * seq_rows, MLA_HEADS * MLA_V)


def _gqa_streams(tq):
    for streams in (GQA_GROUP, 2, 1):
        if (GQA_GROUP // streams * tq) % MXU_DIM == 0:
            return streams
    return 1


def _gqa_kernel(q_ref, k_ref, vt_ref, o_ref, s0_ref, s1_ref, *, n_real, tk):
    g, _, tq, _ = q_ref.shape
    streams = s0_ref.shape[0]
    per = g // streams
    qs = [q_ref[i * per:(i + 1) * per, 0].reshape(per * tq, LANE) for i in range(streams)]
    k_at = lambda i, s, n: k_ref[0, _lane_ds(s, n), :]
    vt_at = lambda i, s, n: vt_ref[0, :, _lane_ds(s, n)]

    def write(outs):
        heads = [o[:, i * tq:(i + 1) * tq] for o in outs for i in range(per)]
        o_ref[0] = jnp.concatenate(heads, axis=0).T.astype(BF16)

    _attend(qs, k_at, vt_at, (s0_ref, s1_ref), n_real, tk, write)


def _gqa_attn(qb, kb, vtb, batch, seq_rows):
    n_real = seq_rows - META_PAD
    tq = _div_tile(seq_rows, 704, LANE // 2)
    streams = _gqa_streams(tq)
    tk = _kv_tile(n_real)
    q4 = qb.reshape(GQA_HEADS, batch, seq_rows, LANE)
    k3 = kb.reshape(batch, seq_rows, LANE)
    width = GQA_GROUP * GQA_HD
    s_buf = pltpu.VMEM((streams, tk, GQA_GROUP // streams * tq), F32)
    return pl.pallas_call(
        functools.partial(_gqa_kernel, n_real=n_real, tk=tk),
        grid=(batch, GQA_KV_HEADS, seq_rows // tq),
        in_specs=[
            pl.BlockSpec((GQA_GROUP, 1, tq, LANE), lambda b, j, i: (j, b, i, 0)),
            pl.BlockSpec((1, seq_rows, LANE), lambda b, j, i: (b, 0, 0)),
            pl.BlockSpec((1, VT_ROWS, seq_rows), lambda b, j, i: (j, 0, b)),
        ],
        out_specs=pl.BlockSpec((1, tq, width), lambda b, j, i: (b, i, j)),
        out_shape=jax.ShapeDtypeStruct((batch, seq_rows, GQA_HEADS * GQA_HD), BF16),
        scratch_shapes=[s_buf, s_buf],
        compiler_params=_params("parallel", "parallel", "parallel"),
        name="gqa_attn",
    )(q4, k3, vtb).reshape(batch * seq_rows, GQA_HEADS * GQA_HD)


def _merge_kernel(h_ref, oa_ref, ob_ref, gmix_ref, wg_ref, bg_ref, wao_ref, wbo_ref, wout_ref, o_ref):
    h = h_ref[...]
    d = h.shape[-1]
    hn = _rms(h, gmix_ref[...]).astype(BF16)
    gates = jax.nn.sigmoid(_dot(hn, wg_ref[...]) + bg_ref[...])
    merged = gates[:, :d] * _dot(oa_ref[...], wao_ref[...]) + gates[:, d:] * _dot(ob_ref[...], wbo_ref[...])
    o_ref[...] = h + _dot(merged.astype(BF16), wout_ref[...])


def _merge(h, oa, ob, lw):
    rows, d = h.shape
    tm = _div_tile(rows, ROW_TILE, LANE)
    row_spec = lambda w: pl.BlockSpec((tm, w), lambda i: (i, 0))
    consts = [lw["g_mix"], lw["w_gates"], lw["b_gate"], lw["w_mla_o"], lw["w_gqa_o"], lw["w_out"]]
    return pl.pallas_call(
        _merge_kernel,
        grid=(rows // tm,),
        in_specs=[row_spec(d), row_spec(oa.shape[1]), row_spec(ob.shape[1])]
        + [_const_spec(c.shape) for c in consts],
        out_specs=row_spec(d),
        out_shape=jax.ShapeDtypeStruct((rows, d), F32),
        compiler_params=_params("parallel"),
        name="merge",
    )(h, oa, ob, *consts)


def _block_diag_mean(groups):
    assert sum(groups) == MXU_DIM
    m = np.zeros((MXU_DIM, MXU_DIM), np.float32)
    c0 = 0
    for gsz in groups:
        m[c0:c0 + gsz, c0:c0 + gsz] = 1.0 / gsz
        c0 += gsz
    return jnp.asarray(m, BF16)


def _rotate_half_swap():
    lanes = np.arange(MXU_DIM)
    src = np.where(lanes % (2 * ROPE_PAIR) < ROPE_PAIR, lanes + ROPE_PAIR, lanes - ROPE_PAIR)
    m = np.zeros((MXU_DIM, MXU_DIM), np.float32)
    m[src, lanes] = 1.0
    return jnp.asarray(m, BF16)


def _rope_angles(pos, dim):
    inv = 1.0 / (ROPE_THETA ** (jnp.arange(0, dim, 2, dtype=jnp.float32) / dim))
    return pos.astype(jnp.float32)[:, None] * inv[None, :]


def _tables(n_real):
    p = jnp.arange(n_real + META_PAD, dtype=jnp.int32)
    real = p < n_real
    meta = (p >= n_real) & (p < n_real + N_META)
    pos_1d = jnp.where(real, p + N_META, jnp.where(meta, p - n_real, 0))
    row = jnp.where(real, p // GRID_W, 0)
    col = jnp.where(real, p % GRID_W, 0)
    a1, ar, ac = _rope_angles(pos_1d, MLA_ROPE), _rope_angles(row, AXIAL_HALF), _rope_angles(col, AXIAL_HALF)
    ones = jnp.ones((p.shape[0], MLA_NOPE), F32)
    pad1 = jnp.ones((p.shape[0], LANE - MLA_QK), F32)
    cos_a = jnp.concatenate([ones, jnp.cos(a1), jnp.cos(a1), pad1], axis=1)
    sin_a = jnp.concatenate([0 * ones, -jnp.sin(a1), jnp.sin(a1), 0 * pad1], axis=1)
    cos_b = jnp.tile(jnp.concatenate([jnp.cos(ar), jnp.cos(ar), jnp.cos(ac), jnp.cos(ac)], axis=1), (1, 2))
    sin_b = jnp.tile(jnp.concatenate([-jnp.sin(ar), jnp.sin(ar), -jnp.sin(ac), jnp.sin(ac)], axis=1), (1, 2))
    return dict(
        cos_a=cos_a, sin_a=sin_a, cos_b=cos_b, sin_b=sin_b,
        bd_a=_block_diag_mean([MLA_NOPE, MLA_ROPE, LANE - MLA_QK] * 2),
        bd_c=_block_diag_mean([GQA_HD] * 4),
        bd_m=_block_diag_mean([MLA_NOPE, MLA_ROPE, LANE - MLA_QK, GQA_HD, GQA_HD]),
        swap=_rotate_half_swap(),
    )


def _prep_layer(ffn1_norm, ffn1_w13, ffn1_w2, mix_norm, w_in, b_gate, q_a_norm, kv_a_norm, w_uq, w_ukv,
                mla_q_norm, mla_k_norm, gqa_q_norm, gqa_k_norm, w_mla_o, w_gqa_o, w_out,
                ffn2_norm, ffn2_w13, ffn2_w2, chunks):
    d = w_in.shape[0]
    d_ff = ffn1_w2.shape[0]

    def w13r(w13):
        pieces = []
        for c0, w in chunks:
            pieces += [w13[:, c0:c0 + w], w13[:, d_ff + c0:d_ff + c0 + w]]
        return jnp.concatenate(pieces, axis=1).astype(BF16)

    cuts = np.cumsum([0, Q_LORA, KV_LORA, MLA_ROPE, GQA_HEADS * GQA_HD, GQA_KV_HEADS * GQA_HD,
                      GQA_KV_HEADS * GQA_HD]).tolist()
    w_cq, w_ckv, w_kr, w_qg, w_kg, w_vg = (w_in[:, cuts[i]:cuts[i + 1]] for i in range(6))
    w_gates = w_in[:, cuts[6]:]
    kr_cols = [jnp.zeros((d, MLA_NOPE), F32), w_kr, jnp.zeros((d, LANE - MLA_QK), F32)]
    w_in_s = jnp.concatenate([w_qg] + kr_cols + [w_kg, w_vg, w_cq, w_ckv], axis=1).astype(BF16)
    assert w_in_s.shape[1] == _C_END

    uq_cols, ukv_cols, v_cols = [], [], []
    for hd in range(MLA_HEADS):
        uq_cols += [w_uq[:, hd * MLA_QK:(hd + 1) * MLA_QK], jnp.zeros((Q_LORA, LANE - MLA_QK), F32)]
        base = hd * (MLA_NOPE + MLA_V)
        ukv_cols += [w_ukv[:, base:base + MLA_NOPE]]
        v_cols += [w_ukv[:, base + MLA_NOPE:base + MLA_NOPE + MLA_V]]
    w_uq_p = jnp.concatenate(uq_cols, axis=1).astype(BF16)
    w_ukv_p = jnp.concatenate(ukv_cols + v_cols, axis=1).astype(BF16)

    zq = jnp.zeros((LANE - MLA_QK,), F32)
    ga_q = jnp.tile(jnp.concatenate([mla_q_norm, zq]), MLA_HEADS) * (MLA_QK ** -0.5 * LOG2E)
    ga_k = jnp.tile(mla_k_norm[:MLA_NOPE], MLA_HEADS)
    ga_kr = jnp.concatenate([jnp.zeros((MLA_NOPE,), F32), mla_k_norm[MLA_NOPE:], zq])
    gb_q = jnp.tile(gqa_q_norm, GQA_HEADS) * (GQA_HD ** -0.5 * LOG2E)
    gb_k = jnp.tile(gqa_k_norm, GQA_KV_HEADS)
    row = lambda v: v.reshape(1, -1).astype(F32)
    return dict(
        g_ffn1=row(ffn1_norm), w13r_1=w13r(ffn1_w13), w2_1=ffn1_w2.astype(BF16),
        g_ffn2=row(ffn2_norm), w13r_2=w13r(ffn2_w13), w2_2=ffn2_w2.astype(BF16),
        g_mix=row(mix_norm), w_in_s=w_in_s, g_q=row(q_a_norm), g_kv=row(kv_a_norm),
        w_uq_p=w_uq_p, w_ukv_p=w_ukv_p,
        ga_q=row(ga_q), ga_k=row(ga_k), ga_kr=row(ga_kr), gb_q=row(gb_q), gb_k=row(gb_k),
        w_gates=w_gates.astype(BF16), b_gate=row(b_gate),
        w_mla_o=w_mla_o.astype(BF16), w_gqa_o=w_gqa_o.astype(BF16), w_out=w_out.astype(BF16),
    )


def _trunk(x, meta_tokens, layers, chunks):
    batch, n_real, d = x.shape
    assert n_real % GRID_W == 0 and n_real % (8 * LANE) == 0
    seq_rows = n_real + META_PAD
    tail = jnp.concatenate([meta_tokens.astype(x.dtype), jnp.zeros((META_PAD - N_META, d), x.dtype)], axis=0)
    h = jnp.concatenate([x, jnp.broadcast_to(tail[None], (batch, META_PAD, d))], axis=1)
    h = h.reshape(batch * seq_rows, d)
    tabs = _tables(n_real)
    for idx, lw in enumerate(layers):
        h = _ffn(h, lw["g_ffn1"], lw["w13r_1"], lw["w2_1"], chunks)
        qa, ka, vta, qb, kb, vtb = _proj(h, lw, tabs, seq_rows)
        oa = _mla_attn(qa, ka, vta, batch, seq_rows)
        ob = _gqa_attn(qb, kb, vtb, batch, seq_rows)
        h = _merge(h, oa, ob, lw)
        real_only = (batch, seq_rows, n_real) if idx == len(layers) - 1 else None
        h = _ffn(h, lw["g_ffn2"], lw["w13r_2"], lw["w2_2"], chunks, real_only)
    return h


def kernel(x_prompt, x_sample, meta_tokens, ffn1_norm, ffn1_w13, ffn1_w2, mix_norm, w_in, b_gate, q_a_norm,
           kv_a_norm, w_uq, w_ukv, mla_q_norm, mla_k_norm, gqa_q_norm, gqa_k_norm, w_mla_o, w_gqa_o, w_out,
           ffn2_norm, ffn2_w13, ffn2_w2):
    per_layer = (ffn1_norm, ffn1_w13, ffn1_w2, mix_norm, w_in, b_gate, q_a_norm, kv_a_norm, w_uq, w_ukv,
                 mla_q_norm, mla_k_norm, gqa_q_norm, gqa_k_norm, w_mla_o, w_gqa_o, w_out,
                 ffn2_norm, ffn2_w13, ffn2_w2)
    chunks = _ff_chunks(ffn1_w2.shape[1])
    layers = [_prep_layer(*(p[l] for p in per_layer), chunks=chunks) for l in range(ffn1_norm.shape[0])]
    return (_trunk(x_prompt, meta_tokens, layers, chunks), _trunk(x_sample, meta_tokens, layers, chunks))
```

```python
import functools
import math

import numpy as np
import jax
import jax.numpy as jnp
from jax import lax
from jax.experimental import pallas as pl
from jax.experimental.pallas import tpu as pltpu

N_META = 16
GRID_W = 64
EPS = 1e-6
ROPE_THETA = 10000.0
MLA_HEADS = 8
MLA_NOPE = 64
MLA_ROPE = 32
MLA_QK = MLA_NOPE + MLA_ROPE
MLA_V = 64
Q_LORA = 384
KV_LORA = 256
GQA_HEADS = 8
GQA_KV_HEADS = 2
GQA_GROUP = GQA_HEADS // GQA_KV_HEADS
GQA_HD = 64
AXIAL_HALF = GQA_HD // 2
ROPE_PAIR = 16
HEAD_V = 64

LANE = 128
MXU_DIM = 256
BF16_ROWS = 16
VMEM_LIMIT = 56 * 1024 * 1024

META_PAD = LANE
VT_ROWS = HEAD_V + BF16_ROWS
KV_BLOCK = 512
ATTN_UNROLL = 4
MAX_RISE = 24.0
ROW_TILE = 1024
LOG2E = math.log2(math.e)
NEG_BIG = -1e30

BF16 = jnp.bfloat16
F32 = jnp.float32


def _div_tile(n, target, mult):
    best = None
    for t in range(mult, min(n, target) + 1, mult):
        if n % t == 0:
            best = t
    assert best is not None, (n, target, mult)
    return best


def _ff_chunks(d_ff):
    chunks, c0 = [], 0
    while c0 < d_ff:
        w = min(3 * MXU_DIM, d_ff - c0)
        chunks.append((c0, w))
        c0 += w
    return tuple(chunks)


def _dot(a, b):
    return jnp.dot(a, b, preferred_element_type=F32)


def _rms(x, g):
    return x * lax.rsqrt(jnp.mean(x * x, axis=-1, keepdims=True) + EPS) * g


def _const_spec(shape):
    nd = len(shape)
    return pl.BlockSpec(shape, lambda *_: (0,) * nd, pipeline_mode=pl.Buffered(1))


def _params(*sem):
    return pltpu.CompilerParams(dimension_semantics=sem, vmem_limit_bytes=VMEM_LIMIT)


def _ffn_kernel(h_ref, g_ref, w13_ref, w2_ref, o_ref, *, chunks):
    h = h_ref[...]
    xn = _rms(h, g_ref[...]).astype(BF16)
    acc = None
    off = 0
    for c0, w in chunks:
        gu = _dot(xn, w13_ref[:, off:off + 2 * w])
        gate, up = gu[:, :w], gu[:, w:]
        a = (gate * jax.nn.sigmoid(gate) * up).astype(BF16)
        part = _dot(a, w2_ref[c0:c0 + w, :])
        acc = part if acc is None else acc + part
        off += 2 * w
    o_ref[...] = h + 0.5 * acc


def _ffn(h, g, w13r, w2, chunks, real_only=None):
    rows, d = h.shape
    if real_only is None:
        tm = _div_tile(rows, ROW_TILE, LANE)
        grid, sem = (rows // tm,), ("parallel",)
        spec = pl.BlockSpec((tm, d), lambda i: (i, 0))
        out_shape = jax.ShapeDtypeStruct((rows, d), F32)
    else:
        batch, seq_rows, n_real = real_only
        tm = _div_tile(n_real, ROW_TILE, LANE)
        h = h.reshape(batch, seq_rows, d)
        grid, sem = (batch, n_real // tm), ("parallel", "parallel")
        spec = pl.BlockSpec((None, tm, d), lambda b, i: (b, i, 0))
        out_shape = jax.ShapeDtypeStruct((batch, n_real, d), F32)
    return pl.pallas_call(
        functools.partial(_ffn_kernel, chunks=chunks),
        grid=grid,
        in_specs=[spec, _const_spec(g.shape), _const_spec(w13r.shape), _const_spec(w2.shape)],
        out_specs=spec,
        out_shape=out_shape,
        compiler_params=_params(*sem),
        name="ffn",
    )(h, g, w13r, w2)


_C_QG = 0
_C_KR = 512
_C_KG = 640
_C_VG = 768
_C_CQ = 896
_C_CKV = 1280
_C_END = 1536


def _group_ms(x, bd):
    return _dot((x * x).astype(BF16), bd)


def _rope(y, cos, sin, swap):
    return y * cos + _dot(y.astype(BF16), swap) * sin


def _store_vt(vt_ref, head0, v_pair, ones_tile):
    vt = v_pair.T
    for k in range(2):
        vt_ref[head0 + k, :HEAD_V, :] = vt[k * HEAD_V:(k + 1) * HEAD_V].astype(BF16)
        vt_ref[head0 + k, HEAD_V:, :] = ones_tile


def _proj_kernel(h_ref, gmix_ref, win_ref, gq_ref, gkv_ref, wuq_ref, wukv_ref,
                 bd_a_ref, bd_c_ref, bd_m_ref, swap_ref, gaq_ref, gak_ref, gakr_ref, gbq_ref, gbk_ref,
                 cos_a_ref, sin_a_ref, cos_b_ref, sin_b_ref,
                 qa_ref, ka_ref, vta_ref, qb_ref, kb_ref, vtb_ref):
    tm = h_ref.shape[0]
    hn = _rms(h_ref[...], gmix_ref[...]).astype(BF16)
    z = _dot(hn, win_ref[...])
    cq = _rms(z[:, _C_CQ:_C_CKV], gq_ref[...]).astype(BF16)
    ckv = _rms(z[:, _C_CKV:_C_END], gkv_ref[...]).astype(BF16)
    qa = _dot(cq, wuq_ref[...])
    kva = _dot(ckv, wukv_ref[...])
    n_kn = MLA_HEADS * MLA_NOPE

    low_half = lax.broadcasted_iota(jnp.int32, (tm, LANE), 1) < HEAD_V
    cos_a, sin_a = cos_a_ref[...], sin_a_ref[...]
    cos_b, sin_b = cos_b_ref[...], sin_b_ref[...]
    pair = lambda left, right: jnp.concatenate([left, right], axis=1)
    bd_a, bd_c, bd_m, swap = bd_a_ref[...], bd_c_ref[...], bd_m_ref[...], swap_ref[...]
    ones_tile = (lax.broadcasted_iota(jnp.int32, (BF16_ROWS, tm), 0) == 0).astype(F32).astype(BF16)

    x = z[:, _C_KR:_C_VG]
    y = x * lax.rsqrt(_group_ms(x, bd_m) + EPS) * pair(gakr_ref[...], gbk_ref[...])
    y = _rope(y, pair(cos_a, cos_b), pair(sin_a, sin_b), swap)
    kr = y[:, :LANE]
    kb_ref[...] = y[:, LANE:].astype(BF16)
    _store_vt(vtb_ref, 0, z[:, _C_VG:_C_CQ], ones_tile)
    for c in range(MLA_HEADS // 2):
        v0 = n_kn + c * LANE
        _store_vt(vta_ref, 2 * c, kva[:, v0:v0 + LANE], ones_tile)

    cos_aa, sin_aa = pair(cos_a, cos_a), pair(sin_a, sin_a)
    for c in range(MLA_HEADS * LANE // MXU_DIM):
        sl = slice(c * MXU_DIM, (c + 1) * MXU_DIM)
        x = qa[:, sl]
        y = _rope(x * lax.rsqrt(_group_ms(x, bd_a) + EPS) * gaq_ref[:, sl], cos_aa, sin_aa, swap)
        for half in range(2):
            qa_ref[2 * c + half] = y[:, half * LANE:(half + 1) * LANE].astype(BF16)

    for c in range(n_kn // MXU_DIM):
        sl = slice(c * MXU_DIM, (c + 1) * MXU_DIM)
        xk = kva[:, sl]
        yk = xk * lax.rsqrt(_group_ms(xk, bd_c) + EPS) * gak_ref[:, sl]
        for half in range(2):
            t = yk[:, half * LANE:(half + 1) * LANE]
            for k, tile in enumerate((t, pltpu.roll(t, HEAD_V, 1))):
                ka_ref[4 * c + 2 * half + k] = (jnp.where(low_half, tile, 0.0) + kr).astype(BF16)

    cos_bb, sin_bb = pair(cos_b, cos_b), pair(sin_b, sin_b)
    for kv in range(GQA_KV_HEADS):
        sl = slice(kv * MXU_DIM, (kv + 1) * MXU_DIM)
        xg = z[:, sl]
        yg = _rope(xg * lax.rsqrt(_group_ms(xg, bd_c) + EPS) * gbq_ref[:, sl], cos_bb, sin_bb, swap)
        keep = low_half if kv == 0 else jnp.logical_not(low_half)
        for half in range(2):
            t = yg[:, half * LANE:(half + 1) * LANE]
            tiles = (t, pltpu.roll(t, GQA_HD, 1)) if kv == 0 else (pltpu.roll(t, GQA_HD, 1), t)
            for k in range(2):
                head = GQA_GROUP * kv + 2 * half + k
                qb_ref[head] = jnp.where(keep, tiles[k], 0.0).astype(BF16)


def _proj(h, lw, tabs, seq_rows):
    rows, d = h.shape
    tm = _div_tile(seq_rows, 640, LANE)
    per_seq = seq_rows // tm
    row_spec = lambda w: pl.BlockSpec((tm, w), lambda i: (i, 0))
    tab_spec = pl.BlockSpec((tm, LANE), lambda i: (i % per_seq, 0))
    head_spec = pl.BlockSpec((MLA_HEADS, tm, LANE), lambda i: (0, i, 0))
    vt_spec = lambda n: pl.BlockSpec((n, VT_ROWS, tm), lambda i: (0, 0, i))
    consts = [lw["g_mix"], lw["w_in_s"], lw["g_q"], lw["g_kv"], lw["w_uq_p"], lw["w_ukv_p"],
              tabs["bd_a"], tabs["bd_c"], tabs["bd_m"], tabs["swap"],
              lw["ga_q"], lw["ga_k"], lw["ga_kr"], lw["gb_q"], lw["gb_k"]]
    heads = jax.ShapeDtypeStruct((MLA_HEADS, rows, LANE), BF16)
    return pl.pallas_call(
        _proj_kernel,
        grid=(rows // tm,),
        in_specs=[row_spec(d)] + [_const_spec(c.shape) for c in consts] + [tab_spec] * 4,
        out_specs=[head_spec, head_spec, vt_spec(MLA_HEADS), head_spec, row_spec(LANE),
                   vt_spec(GQA_KV_HEADS)],
        out_shape=[heads, heads, jax.ShapeDtypeStruct((MLA_HEADS, VT_ROWS, rows), BF16), heads,
                   jax.ShapeDtypeStruct((rows, LANE), BF16),
                   jax.ShapeDtypeStruct((GQA_KV_HEADS, VT_ROWS, rows), BF16)],
        compiler_params=_params("parallel"),
        name="proj",
    )(h, *consts, tabs["cos_a"], tabs["sin_a"], tabs["cos_b"], tabs["sin_b"])


def _lane_ds(start, size):
    return pl.ds(start if isinstance(start, int) else pl.multiple_of(start, LANE), size)


def _col_max(s):
    return jnp.max(s, axis=0, keepdims=True)


def _flash_update(s, s_max, vt, carry):
    m, acc = carry
    m_new = jnp.maximum(m, s_max)
    p = jnp.exp2(s - m_new).astype(BF16)
    acc = jnp.exp2(m - m_new) * acc + _dot(vt, p)
    return m_new, acc


def _scores_fn(qs, k_at):
    def scores(i, start, size):
        return lax.dot_general(k_at(i, start, size), qs[i], (((1,), (1,)), ((), ())),
                               preferred_element_type=F32)
    return scores


def _looped_blocks(n_blocks, unroll):
    return n_blocks // unroll * unroll if n_blocks >= 2 * unroll else 0


def _meta_mask(nq):
    return lax.broadcasted_iota(jnp.int32, (META_PAD, nq), 0) < N_META


def _flash_one_pass(qs, k_at, vt_at, n_real, tk):
    n = len(qs)
    n_blk = n_real // tk
    assert n_real % tk == 0
    scores = _scores_fn(qs, k_at)

    def fold(i, start, size, carry, mask=None):
        m, acc, rise = carry
        s = scores(i, start, size)
        if mask is not None:
            s = jnp.where(mask, s, NEG_BIG)
        s_max = _col_max(s)
        acc = acc + _dot(vt_at(i, start, size), jnp.exp2(s - m).astype(BF16))
        m_new = jnp.maximum(m, s_max)
        return m_new, acc * jnp.exp2(m - m_new), jnp.maximum(rise, s_max - m)

    def sweep(first, count, carries):
        for k in range(count):
            carries = [fold(i, (first + k) * tk, tk, carries[i]) for i in range(n)]
        return carries

    carries = []
    for i in range(n):
        s = scores(i, 0, tk)
        m = _col_max(s)
        carries.append((m, _dot(vt_at(i, 0, tk), jnp.exp2(s - m).astype(BF16)), jnp.zeros_like(m)))
    looped = _looped_blocks(n_blk - 1, ATTN_UNROLL)
    if looped:
        carries = lax.fori_loop(0, looped // ATTN_UNROLL,
                                lambda t, c: sweep(1 + t * ATTN_UNROLL, ATTN_UNROLL, c), carries)
    carries = sweep(1 + looped, n_blk - 1 - looped, carries)
    outs, rise = [], None
    for i in range(n):
        _, acc, r = fold(i, n_real, META_PAD, carries[i], _meta_mask(qs[i].shape[0]))
        outs.append(acc[:HEAD_V] / acc[HEAD_V:HEAD_V + 1])
        rise = jnp.max(r) if rise is None else jnp.maximum(rise, jnp.max(r))
    return outs, rise


def _attend(qs, k_at, vt_at, s_refs, n_real, tk, write):
    outs, rise = _flash_one_pass(qs, k_at, vt_at, n_real, tk)
    write(outs)

    @pl.when(rise > MAX_RISE)
    def _():
        write(_flash(qs, k_at, vt_at, s_refs, n_real, tk))


def _flash(qs, k_at, vt_at, s_refs, n_real, tk):
    n = len(qs)
    n_blk = n_real // tk
    assert n_real % tk == 0
    scores = _scores_fn(qs, k_at)

    def issue(blk, s_buf):
        maxes = []
        for i in range(n):
            s = scores(i, blk * tk, tk)
            s_buf[i] = s
            maxes.append(_col_max(s))
        return maxes

    def stage(blk, parity, state, issue_next=True):
        carries, maxes = state
        next_maxes = issue(blk + 1, s_refs[1 - parity]) if issue_next else maxes
        carries = [_flash_update(s_refs[parity][i], maxes[i], vt_at(i, blk * tk, tk), carries[i])
                   for i in range(n)]
        return carries, next_maxes

    carries = [(jnp.full((1, q.shape[0]), NEG_BIG, F32), jnp.zeros((VT_ROWS, q.shape[0]), F32)) for q in qs]
    state = (carries, issue(0, s_refs[0]))
    looped = _looped_blocks(n_blk - 1, 2)
    if looped:
        def body(t, state):
            return stage(2 * t + 1, 1, stage(2 * t, 0, state))
        state = lax.fori_loop(0, looped // 2, body, state)
    for blk in range(looped, n_blk):
        state = stage(blk, blk % 2, state, issue_next=blk + 1 < n_blk)
    carries, _ = state
    outs = []
    for i in range(n):
        s_meta = jnp.where(_meta_mask(qs[i].shape[0]), scores(i, n_real, META_PAD), NEG_BIG)
        _, acc = _flash_update(s_meta, _col_max(s_meta), vt_at(i, n_real, META_PAD), carries[i])
        outs.append(acc[:HEAD_V] / acc[HEAD_V:HEAD_V + 1])
    return outs


def _mla_kernel(q_ref, k_ref, vt_ref, o_ref, s0_ref, s1_ref, *, n_real, tk):
    qs = [q_ref[0, 0], q_ref[1, 0]]
    k_at = lambda i, s, n: k_ref[i, 0, _lane_ds(s, n), :]
    vt_at = lambda i, s, n: vt_ref[i, :, _lane_ds(s, n)]

    def write(outs):
        o_ref[0] = jnp.concatenate(outs, axis=0).T.astype(BF16)

    _attend(qs, k_at, vt_at, (s0_ref, s1_ref), n_real, tk, write)


def _kv_tile(n_real):
    return _div_tile(n_real, KV_BLOCK, LANE)


def _mla_attn(qa, ka, vta, batch, seq_rows):
    n_real = seq_rows - META_PAD
    tq = _div_tile(seq_rows, 1792, LANE)
    tk = _kv_tile(n_real)
    q4 = qa.reshape(MLA_HEADS, batch, seq_rows, LANE)
    k4 = ka.reshape(MLA_HEADS, batch, seq_rows, LANE)
    return pl.pallas_call(
        functools.partial(_mla_kernel, n_real=n_real, tk=tk),
        grid=(batch, MLA_HEADS // 2, seq_rows // tq),
        in_specs=[
            pl.BlockSpec((2, 1, tq, LANE), lambda b, j, i: (j, b, i, 0)),
            pl.BlockSpec((2, 1, seq_rows, LANE), lambda b, j, i: (j, b, 0, 0)),
            pl.BlockSpec((2, VT_ROWS, seq_rows), lambda b, j, i: (j, 0, b)),
        ],
        out_specs=pl.BlockSpec((1, tq, LANE), lambda b, j, i: (b, i, j)),
        out_shape=jax.ShapeDtypeStruct((batch, seq_rows, MLA_HEADS * MLA_V), BF16),
        scratch_shapes=[pltpu.VMEM((2, tk, tq), F32), pltpu.VMEM((2, tk, tq), F32)],
        compiler_params=_params("parallel", "parallel", "parallel"),
        name="mla_attn",
    )(q4, k4, vta).reshape(batch * seq_rows, MLA_HEADS * MLA_V)


def _gqa_streams(tq):
    for streams in (GQA_GROUP, 2, 1):
        if (GQA_GROUP // streams * tq) % MXU_DIM == 0:
            return streams
    return 1


def _gqa_kernel(q_ref, k_ref, vt_ref, o_ref, s0_ref, s1_ref, *, n_real, tk):
    g, _, tq, _ = q_ref.shape
    streams = s0_ref.shape[0]
    per = g // streams
    qs = [q_ref[i * per:(i + 1) * per, 0].reshape(per * tq, LANE) for i in range(streams)]
    k_at = lambda i, s, n: k_ref[0, _lane_ds(s, n), :]
    vt_at = lambda i, s, n: vt_ref[0, :, _lane_ds(s, n)]

    def write(outs):
        heads = [o[:, i * tq:(i + 1) * tq] for o in outs for i in range(per)]
        o_ref[0] = jnp.concatenate(heads, axis=0).T.astype(BF16)

    _attend(qs, k_at, vt_at, (s0_ref, s1_ref), n_real, tk, write)


def _gqa_attn(qb, kb, vtb, batch, seq_rows):
    n_real = seq_rows - META_PAD
    tq = _div_tile(seq_rows, 704, LANE // 2)
    streams = _gqa_streams(tq)
    tk = _kv_tile(n_real)
    q4 = qb.reshape(GQA_HEADS, batch, seq_rows, LANE)
    k3 = kb.reshape(batch, seq_rows, LANE)
    width = GQA_GROUP * GQA_HD
    s_buf = pltpu.VMEM((streams, tk, GQA_GROUP // streams * tq), F32)
    return pl.pallas_call(
        functools.partial(_gqa_kernel, n_real=n_real, tk=tk),
        grid=(batch, GQA_KV_HEADS, seq_rows // tq),
        in_specs=[
            pl.BlockSpec((GQA_GROUP, 1, tq, LANE), lambda b, j, i: (j, b, i, 0)),
            pl.BlockSpec((1, seq_rows, LANE), lambda b, j, i: (b, 0, 0)),
            pl.BlockSpec((1, VT_ROWS, seq_rows), lambda b, j, i: (j, 0, b)),
        ],
        out_specs=pl.BlockSpec((1, tq, width), lambda b, j, i: (b, i, j)),
        out_shape=jax.ShapeDtypeStruct((batch, seq_rows, GQA_HEADS * GQA_HD), BF16),
        scratch_shapes=[s_buf, s_buf],
        compiler_params=_params("parallel", "parallel", "parallel"),
        name="gqa_attn",
    )(q4, k3, vtb).reshape(batch * seq_rows, GQA_HEADS * GQA_HD)


def _merge_kernel(h_ref, oa_ref, ob_ref, gmix_ref, wg_ref, bg_ref, wao_ref, wbo_ref, wout_ref, o_ref):
    h = h_ref[...]
    d = h.shape[-1]
    hn = _rms(h, gmix_ref[...]).astype(BF16)
    gates = jax.nn.sigmoid(_dot(hn, wg_ref[...]) + bg_ref[...])
    merged = gates[:, :d] * _dot(oa_ref[...], wao_ref[...]) + gates[:, d:] * _dot(ob_ref[...], wbo_ref[...])
    o_ref[...] = h + _dot(merged.astype(BF16), wout_ref[...])


def _merge(h, oa, ob, lw):
    rows, d = h.shape
    tm = _div_tile(rows, ROW_TILE, LANE)
    row_spec = lambda w: pl.BlockSpec((tm, w), lambda i: (i, 0))
    consts = [lw["g_mix"], lw["w_gates"], lw["b_gate"], lw["w_mla_o"], lw["w_gqa_o"], lw["w_out"]]
    return pl.pallas_call(
        _merge_kernel,
        grid=(rows // tm,),
        in_specs=[row_spec(d), row_spec(oa.shape[1]), row_spec(ob.shape[1])]
        + [_const_spec(c.shape) for c in consts],
        out_specs=row_spec(d),
        out_shape=jax.ShapeDtypeStruct((rows, d), F32),
        compiler_params=_params("parallel"),
        name="merge",
    )(h, oa, ob, *consts)


def _block_diag_mean(groups):
    assert sum(groups) == MXU_DIM
    m = np.zeros((MXU_DIM, MXU_DIM), np.float32)
    c0 = 0
    for gsz in groups:
        m[c0:c0 + gsz, c0:c0 + gsz] = 1.0 / gsz
        c0 += gsz
    return jnp.asarray(m, BF16)


def _rotate_half_swap():
    lanes = np.arange(MXU_DIM)
    src = np.where(lanes % (2 * ROPE_PAIR) < ROPE_PAIR, lanes + ROPE_PAIR, lanes - ROPE_PAIR)
    m = np.zeros((MXU_DIM, MXU_DIM), np.float32)
    m[src, lanes] = 1.0
    return jnp.asarray(m, BF16)


def _rope_angles(pos, dim):
    inv = 1.0 / (ROPE_THETA ** (jnp.arange(0, dim, 2, dtype=jnp.float32) / dim))
    return pos.astype(jnp.float32)[:, None] * inv[None, :]


def _tables(n_real):
    p = jnp.arange(n_real + META_PAD, dtype=jnp.int32)
    real = p < n_real
    meta = (p >= n_real) & (p < n_real + N_META)
    pos_1d = jnp.where(real, p + N_META, jnp.where(meta, p - n_real, 0))
    row = jnp.where(real, p // GRID_W, 0)
    col = jnp.where(real, p % GRID_W, 0)
    a1, ar, ac = _rope_angles(pos_1d, MLA_ROPE), _rope_angles(row, AXIAL_HALF), _rope_angles(col, AXIAL_HALF)
    ones = jnp.ones((p.shape[0], MLA_NOPE), F32)
    pad1 = jnp.ones((p.shape[0], LANE - MLA_QK), F32)
    cos_a = jnp.concatenate([ones, jnp.cos(a1), jnp.cos(a1), pad1], axis=1)
    sin_a = jnp.concatenate([0 * ones, -jnp.sin(a1), jnp.sin(a1), 0 * pad1], axis=1)
    cos_b = jnp.tile(jnp.concatenate([jnp.cos(ar), jnp.cos(ar), jnp.cos(ac), jnp.cos(ac)], axis=1), (1, 2))
    sin_b = jnp.tile(jnp.concatenate([-jnp.sin(ar), jnp.sin(ar), -jnp.sin(ac), jnp.sin(ac)], axis=1), (1, 2))
    return dict(
        cos_a=cos_a, sin_a=sin_a, cos_b=cos_b, sin_b=sin_b,
        bd_a=_block_diag_mean([MLA_NOPE, MLA_ROPE, LANE - MLA_QK] * 2),
        bd_c=_block_diag_mean([GQA_HD] * 4),
        bd_m=_block_diag_mean([MLA_NOPE, MLA_ROPE, LANE - MLA_QK, GQA_HD, GQA_HD]),
        swap=_rotate_half_swap(),
    )


def _prep_layer(ffn1_norm, ffn1_w13, ffn1_w2, mix_norm, w_in, b_gate, q_a_norm, kv_a_norm, w_uq, w_ukv,
                mla_q_norm, mla_k_norm, gqa_q_norm, gqa_k_norm, w_mla_o, w_gqa_o, w_out,
                ffn2_norm, ffn2_w13, ffn2_w2, chunks):
    d = w_in.shape[0]
    d_ff = ffn1_w2.shape[0]

    def w13r(w13):
        pieces = []
        for c0, w in chunks:
            pieces += [w13[:, c0:c0 + w], w13[:, d_ff + c0:d_ff + c0 + w]]
        return jnp.concatenate(pieces, axis=1).astype(BF16)

    cuts = np.cumsum([0, Q_LORA, KV_LORA, MLA_ROPE, GQA_HEADS * GQA_HD, GQA_KV_HEADS * GQA_HD,
                      GQA_KV_HEADS * GQA_HD]).tolist()
    w_cq, w_ckv, w_kr, w_qg, w_kg, w_vg = (w_in[:, cuts[i]:cuts[i + 1]] for i in range(6))
    w_gates = w_in[:, cuts[6]:]
    kr_cols = [jnp.zeros((d, MLA_NOPE), F32), w_kr, jnp.zeros((d, LANE - MLA_QK), F32)]
    w_in_s = jnp.concatenate([w_qg] + kr_cols + [w_kg, w_vg, w_cq, w_ckv], axis=1).astype(BF16)
    assert w_in_s.shape[1] == _C_END

    uq_cols, ukv_cols, v_cols = [], [], []
    for hd in range(MLA_HEADS):
        uq_cols += [w_uq[:, hd * MLA_QK:(hd + 1) * MLA_QK], jnp.zeros((Q_LORA, LANE - MLA_QK), F32)]
        base = hd * (MLA_NOPE + MLA_V)
        ukv_cols += [w_ukv[:, base:base + MLA_NOPE]]
        v_cols += [w_ukv[:, base + MLA_NOPE:base + MLA_NOPE + MLA_V]]
    w_uq_p = jnp.concatenate(uq_cols, axis=1).astype(BF16)
    w_ukv_p = jnp.concatenate(ukv_cols + v_cols, axis=1).astype(BF16)

    zq = jnp.zeros((LANE - MLA_QK,), F32)
    ga_q = jnp.tile(jnp.concatenate([mla_q_norm, zq]), MLA_HEADS) * (MLA_QK ** -0.5 * LOG2E)
    ga_k = jnp.tile(mla_k_norm[:MLA_NOPE], MLA_HEADS)
    ga_kr = jnp.concatenate([jnp.zeros((MLA_NOPE,), F32), mla_k_norm[MLA_NOPE:], zq])
    gb_q = jnp.tile(gqa_q_norm, GQA_HEADS) * (GQA_HD ** -0.5 * LOG2E)
    gb_k = jnp.tile(gqa_k_norm, GQA_KV_HEADS)
    row = lambda v: v.reshape(1, -1).astype(F32)
    return dict(
        g_ffn1=row(ffn1_norm), w13r_1=w13r(ffn1_w13), w2_1=ffn1_w2.astype(BF16),
        g_ffn2=row(ffn2_norm), w13r_2=w13r(ffn2_w13), w2_2=ffn2_w2.astype(BF16),
        g_mix=row(mix_norm), w_in_s=w_in_s, g_q=row(q_a_norm), g_kv=row(kv_a_norm),
        w_uq_p=w_uq_p, w_ukv_p=w_ukv_p,
        ga_q=row(ga_q), ga_k=row(ga_k), ga_kr=row(ga_kr), gb_q=row(gb_q), gb_k=row(gb_k),
        w_gates=w_gates.astype(BF16), b_gate=row(b_gate),
        w_mla_o=w_mla_o.astype(BF16), w_gqa_o=w_gqa_o.astype(BF16), w_out=w_out.astype(BF16),
    )


def _trunk(x, meta_tokens, layers, chunks):
    batch, n_real, d = x.shape
    assert n_real % GRID_W == 0 and n_real % (8 * LANE) == 0
    seq_rows = n_real + META_PAD
    tail = jnp.concatenate([meta_tokens.astype(x.dtype), jnp.zeros((META_PAD - N_META, d), x.dtype)], axis=0)
    h = jnp.concatenate([x, jnp.broadcast_to(tail[None], (batch, META_PAD, d))], axis=1)
    h = h.reshape(batch * seq_rows, d)
    tabs = _tables(n_real)
    for idx, lw in enumerate(layers):
        h = _ffn(h, lw["g_ffn1"], lw["w13r_1"], lw["w2_1"], chunks)
        qa, ka, vta, qb, kb, vtb = _proj(h, lw, tabs, seq_rows)
        oa = _mla_attn(qa, ka, vta, batch, seq_rows)
        ob = _gqa_attn(qb, kb, vtb, batch, seq_rows)
        h = _merge(h, oa, ob, lw)
        real_only = (batch, seq_rows, n_real) if idx == len(layers) - 1 else None
        h = _ffn(h, lw["g_ffn2"], lw["w13r_2"], lw["w2_2"], chunks, real_only)
    return h


def kernel(x_prompt, x_sample, meta_tokens, ffn1_norm, ffn1_w13, ffn1_w2, mix_norm, w_in, b_gate, q_a_norm,
           kv_a_norm, w_uq, w_ukv, mla_q_norm, mla_k_norm, gqa_q_norm, gqa_k_norm, w_mla_o, w_gqa_o, w_out,
           ffn2_norm, ffn2_w13, ffn2_w2):
    per_layer = (ffn1_norm, ffn1_w13, ffn1_w2, mix_norm, w_in, b_gate, q_a_norm, kv_a_norm, w_uq, w_ukv,
                 mla_q_norm, mla_k_norm, gqa_q_norm, gqa_k_norm, w_mla_o, w_gqa_o, w_out,
                 ffn2_norm, ffn2_w13, ffn2_w2)
    chunks = _ff_chunks(ffn1_w2.shape[1])
    layers = [_prep_layer(*(p[l] for p in per_layer), chunks=chunks) for l in range(ffn1_norm.shape[0])]
    return (_trunk(x_prompt, meta_tokens, layers, chunks), _trunk(x_sample, meta_tokens, layers, chunks))
```

```python
import functools
import math

import numpy as np
import jax
import jax.numpy as jnp
from jax import lax
from jax.experimental import pallas as pl
from jax.experimental.pallas import tpu as pltpu

N_META = 16
GRID_W = 64
EPS = 1e-6
ROPE_THETA = 10000.0
MLA_HEADS = 8
MLA_NOPE = 64
MLA_ROPE = 32
MLA_QK = MLA_NOPE + MLA_ROPE
MLA_V = 64
Q_LORA = 384
KV_LORA = 256
GQA_HEADS = 8
GQA_KV_HEADS = 2
GQA_GROUP = GQA_HEADS // GQA_KV_HEADS
GQA_HD = 64
AXIAL_HALF = GQA_HD // 2
ROPE_PAIR = 16
HEAD_V = 64

LANE = 128
MXU_DIM = 256
BF16_ROWS = 16
VMEM_LIMIT = 56 * 1024 * 1024

META_PAD = LANE
VT_ROWS = HEAD_V + BF16_ROWS
KV_BLOCK = 512
ATTN_UNROLL = 4
MAX_RISE = 24.0
ROW_TILE = 1024
LOG2E = math.log2(math.e)
NEG_BIG = -1e30

BF16 = jnp.bfloat16
F32 = jnp.float32


def _div_tile(n, target, mult):
    best = None
    for t in range(mult, min(n, target) + 1, mult):
        if n % t == 0:
            best = t
    assert best is not None, (n, target, mult)
    return best


def _ff_chunks(d_ff):
    chunks, c0 = [], 0
    while c0 < d_ff:
        w = min(3 * MXU_DIM, d_ff - c0)
        chunks.append((c0, w))
        c0 += w
    return tuple(chunks)


def _dot(a, b):
    return jnp.dot(a, b, preferred_element_type=F32)


def _rms(x, g):
    return x * lax.rsqrt(jnp.mean(x * x, axis=-1, keepdims=True) + EPS) * g


def _const_spec(shape):
    nd = len(shape)
    return pl.BlockSpec(shape, lambda *_: (0,) * nd, pipeline_mode=pl.Buffered(1))


def _params(*sem):
    return pltpu.CompilerParams(dimension_semantics=sem, vmem_limit_bytes=VMEM_LIMIT)


def _ffn_kernel(h_ref, g_ref, w13_ref, w2_ref, o_ref, *, chunks):
    o_ref[...] = _swiglu_residual(h_ref[...], g_ref, w13_ref, w2_ref, chunks)


def _swiglu_residual(h, g_ref, w13_ref, w2_ref, chunks):
    xn = _rms(h, g_ref[...]).astype(BF16)
    acc = None
    off = 0
    for c0, w in chunks:
        gu = _dot(xn, w13_ref[:, off:off + 2 * w])
        gate, up = gu[:, :w], gu[:, w:]
        a = (gate * jax.nn.sigmoid(gate) * up).astype(BF16)
        part = _dot(a, w2_ref[c0:c0 + w, :])
        acc = part if acc is None else acc + part
        off += 2 * w
    return h + 0.5 * acc


def _ffn(h, g, w13r, w2, chunks, real_only=None):
    rows, d = h.shape
    if real_only is None:
        tm = _div_tile(rows, ROW_TILE, LANE)
        grid, sem = (rows // tm,), ("parallel",)
        spec = pl.BlockSpec((tm, d), lambda i: (i, 0))
        out_shape = jax.ShapeDtypeStruct((rows, d), F32)
    else:
        batch, seq_rows, n_real = real_only
        tm = _div_tile(n_real, ROW_TILE, LANE)
        h = h.reshape(batch, seq_rows, d)
        grid, sem = (batch, n_real // tm), ("parallel", "parallel")
        spec = pl.BlockSpec((None, tm, d), lambda b, i: (b, i, 0))
        out_shape = jax.ShapeDtypeStruct((batch, n_real, d), F32)
    return pl.pallas_call(
        functools.partial(_ffn_kernel, chunks=chunks),
        grid=grid,
        in_specs=[spec, _const_spec(g.shape), _const_spec(w13r.shape), _const_spec(w2.shape)],
        out_specs=spec,
        out_shape=out_shape,
        compiler_params=_params(*sem),
        name="ffn",
    )(h, g, w13r, w2)


_C_QG = 0
_C_KR = 512
_C_KG = 640
_C_VG = 768
_C_CQ = 896
_C_CKV = 1280
_C_END = 1536


def _group_ms(x, bd):
    return _dot((x * x).astype(BF16), bd)


def _rope(y, cos, sin, swap):
    return y * cos + _dot(y.astype(BF16), swap) * sin


def _store_vt(vt_ref, head0, v_pair, ones_tile):
    vt = v_pair.T
    for k in range(2):
        vt_ref[head0 + k, :HEAD_V, :] = vt[k * HEAD_V:(k + 1) * HEAD_V].astype(BF16)
        vt_ref[head0 + k, HEAD_V:, :] = ones_tile


def _proj_kernel(h_ref, gmix_ref, win_ref, gq_ref, gkv_ref, wuq_ref, wukv_ref,
                 bd_a_ref, bd_c_ref, bd_m_ref, swap_ref, gaq_ref, gak_ref, gakr_ref, gbq_ref, gbk_ref,
                 cos_a_ref, sin_a_ref, cos_b_ref, sin_b_ref,
                 qa_ref, ka_ref, vta_ref, qb_ref, kb_ref, vtb_ref):
    tm = h_ref.shape[0]
    hn = _rms(h_ref[...], gmix_ref[...]).astype(BF16)
    z = _dot(hn, win_ref[...])
    cq = _rms(z[:, _C_CQ:_C_CKV], gq_ref[...]).astype(BF16)
    ckv = _rms(z[:, _C_CKV:_C_END], gkv_ref[...]).astype(BF16)
    qa = _dot(cq, wuq_ref[...])
    kva = _dot(ckv, wukv_ref[...])
    n_kn = MLA_HEADS * MLA_NOPE

    low_half = lax.broadcasted_iota(jnp.int32, (tm, LANE), 1) < HEAD_V
    cos_a, sin_a = cos_a_ref[...], sin_a_ref[...]
    cos_b, sin_b = cos_b_ref[...], sin_b_ref[...]
    pair = lambda left, right: jnp.concatenate([left, right], axis=1)
    bd_a, bd_c, bd_m, swap = bd_a_ref[...], bd_c_ref[...], bd_m_ref[...], swap_ref[...]
    ones_tile = (lax.broadcasted_iota(jnp.int32, (BF16_ROWS, tm), 0) == 0).astype(F32).astype(BF16)

    x = z[:, _C_KR:_C_VG]
    y = x * lax.rsqrt(_group_ms(x, bd_m) + EPS) * pair(gakr_ref[...], gbk_ref[...])
    y = _rope(y, pair(cos_a, cos_b), pair(sin_a, sin_b), swap)
    kr = y[:, :LANE]
    kb_ref[...] = y[:, LANE:].astype(BF16)
    _store_vt(vtb_ref, 0, z[:, _C_VG:_C_CQ], ones_tile)
    for c in range(MLA_HEADS // 2):
        v0 = n_kn + c * LANE
        _store_vt(vta_ref, 2 * c, kva[:, v0:v0 + LANE], ones_tile)

    cos_aa, sin_aa = pair(cos_a, cos_a), pair(sin_a, sin_a)
    for c in range(MLA_HEADS * LANE // MXU_DIM):
        sl = slice(c * MXU_DIM, (c + 1) * MXU_DIM)
        x = qa[:, sl]
        y = _rope(x * lax.rsqrt(_group_ms(x, bd_a) + EPS) * gaq_ref[:, sl], cos_aa, sin_aa, swap)
        for half in range(2):
            qa_ref[2 * c + half] = y[:, half * LANE:(half + 1) * LANE].astype(BF16)

    for c in range(n_kn // MXU_DIM):
        sl = slice(c * MXU_DIM, (c + 1) * MXU_DIM)
        xk = kva[:, sl]
        yk = xk * lax.rsqrt(_group_ms(xk, bd_c) + EPS) * gak_ref[:, sl]
        for half in range(2):
            t = yk[:, half * LANE:(half + 1) * LANE]
            for k, tile in enumerate((t, pltpu.roll(t, HEAD_V, 1))):
                ka_ref[4 * c + 2 * half + k] = (jnp.where(low_half, tile, 0.0) + kr).astype(BF16)

    cos_bb, sin_bb = pair(cos_b, cos_b), pair(sin_b, sin_b)
    for kv in range(GQA_KV_HEADS):
        sl = slice(kv * MXU_DIM, (kv + 1) * MXU_DIM)
        xg = z[:, sl]
        yg = _rope(xg * lax.rsqrt(_group_ms(xg, bd_c) + EPS) * gbq_ref[:, sl], cos_bb, sin_bb, swap)
        keep = low_half if kv == 0 else jnp.logical_not(low_half)
        for half in range(2):
            t = yg[:, half * LANE:(half + 1) * LANE]
            tiles = (t, pltpu.roll(t, GQA_HD, 1)) if kv == 0 else (pltpu.roll(t, GQA_HD, 1), t)
            for k in range(2):
                head = GQA_GROUP * kv + 2 * half + k
                qb_ref[head] = jnp.where(keep, tiles[k], 0.0).astype(BF16)


def _proj(h, lw, tabs, seq_rows):
    rows, d = h.shape
    tm = _div_tile(seq_rows, 640, LANE)
    per_seq = seq_rows // tm
    row_spec = lambda w: pl.BlockSpec((tm, w), lambda i: (i, 0))
    tab_spec = pl.BlockSpec((tm, LANE), lambda i: (i % per_seq, 0))
    head_spec = pl.BlockSpec((MLA_HEADS, tm, LANE), lambda i: (0, i, 0))
    vt_spec = lambda n: pl.BlockSpec((n, VT_ROWS, tm), lambda i: (0, 0, i))
    consts = [lw["g_mix"], lw["w_in_s"], lw["g_q"], lw["g_kv"], lw["w_uq_p"], lw["w_ukv_p"],
              tabs["bd_a"], tabs["bd_c"], tabs["bd_m"], tabs["swap"],
              lw["ga_q"], lw["ga_k"], lw["ga_kr"], lw["gb_q"], lw["gb_k"]]
    heads = jax.ShapeDtypeStruct((MLA_HEADS, rows, LANE), BF16)
    return pl.pallas_call(
        _proj_kernel,
        grid=(rows // tm,),
        in_specs=[row_spec(d)] + [_const_spec(c.shape) for c in consts] + [tab_spec] * 4,
        out_specs=[head_spec, head_spec, vt_spec(MLA_HEADS), head_spec, row_spec(LANE),
                   vt_spec(GQA_KV_HEADS)],
        out_shape=[heads, heads, jax.ShapeDtypeStruct((MLA_HEADS, VT_ROWS, rows), BF16), heads,
                   jax.ShapeDtypeStruct((rows, LANE), BF16),
                   jax.ShapeDtypeStruct((GQA_KV_HEADS, VT_ROWS, rows), BF16)],
        compiler_params=_params("parallel"),
        name="proj",
    )(h, *consts, tabs["cos_a"], tabs["sin_a"], tabs["cos_b"], tabs["sin_b"])


def _lane_ds(start, size):
    return pl.ds(start if isinstance(start, int) else pl.multiple_of(start, LANE), size)


def _col_max(s):
    return jnp.max(s, axis=0, keepdims=True)


def _flash_update(s, s_max, vt, carry):
    m, acc = carry
    m_new = jnp.maximum(m, s_max)
    p = jnp.exp2(s - m_new).astype(BF16)
    acc = jnp.exp2(m - m_new) * acc + _dot(vt, p)
    return m_new, acc


def _scores_fn(qs, k_at):
    def scores(i, start, size):
        return lax.dot_general(k_at(i, start, size), qs[i], (((1,), (1,)), ((), ())),
                               preferred_element_type=F32)
    return scores


def _looped_blocks(n_blocks, unroll):
    return n_blocks // unroll * unroll if n_blocks >= 2 * unroll else 0


def _meta_mask(nq):
    return lax.broadcasted_iota(jnp.int32, (META_PAD, nq), 0) < N_META


def _flash_one_pass(qs, k_at, vt_at, n_real, tk):
    n = len(qs)
    n_blk = n_real // tk
    assert n_real % tk == 0
    scores = _scores_fn(qs, k_at)

    def fold(i, start, size, carry, mask=None):
        m, acc, rise = carry
        s = scores(i, start, size)
        if mask is not None:
            s = jnp.where(mask, s, NEG_BIG)
        s_max = _col_max(s)
        acc = acc + _dot(vt_at(i, start, size), jnp.exp2(s - m).astype(BF16))
        m_new = jnp.maximum(m, s_max)
        return m_new, acc * jnp.exp2(m - m_new), jnp.maximum(rise, s_max - m)

    def sweep(first, count, carries):
        for k in range(count):
            carries = [fold(i, (first + k) * tk, tk, carries[i]) for i in range(n)]
        return carries

    carries = []
    for i in range(n):
        s = scores(i, 0, tk)
        m = _col_max(s)
        carries.append((m, _dot(vt_at(i, 0, tk), jnp.exp2(s - m).astype(BF16)), jnp.zeros_like(m)))
    looped = _looped_blocks(n_blk - 1, ATTN_UNROLL)
    if looped:
        carries = lax.fori_loop(0, looped // ATTN_UNROLL,
                                lambda t, c: sweep(1 + t * ATTN_UNROLL, ATTN_UNROLL, c), carries)
    carries = sweep(1 + looped, n_blk - 1 - looped, carries)
    outs, rise = [], None
    for i in range(n):
        _, acc, r = fold(i, n_real, META_PAD, carries[i], _meta_mask(qs[i].shape[0]))
        outs.append(acc[:HEAD_V] / acc[HEAD_V:HEAD_V + 1])
        rise = jnp.max(r) if rise is None else jnp.maximum(rise, jnp.max(r))
    return outs, rise


def _attend(qs, k_at, vt_at, s_refs, n_real, tk, write):
    outs, rise = _flash_one_pass(qs, k_at, vt_at, n_real, tk)
    write(outs)

    @pl.when(rise > MAX_RISE)
    def _():
        write(_flash(qs, k_at, vt_at, s_refs, n_real, tk))


def _flash(qs, k_at, vt_at, s_refs, n_real, tk):
    n = len(qs)
    n_blk = n_real // tk
    assert n_real % tk == 0
    scores = _scores_fn(qs, k_at)

    def issue(blk, s_buf):
        maxes = []
        for i in range(n):
            s = scores(i, blk * tk, tk)
            s_buf[i] = s
            maxes.append(_col_max(s))
        return maxes

    def stage(blk, parity, state, issue_next=True):
        carries, maxes = state
        next_maxes = issue(blk + 1, s_refs[1 - parity]) if issue_next else maxes
        carries = [_flash_update(s_refs[parity][i], maxes[i], vt_at(i, blk * tk, tk), carries[i])
                   for i in range(n)]
        return carries, next_maxes

    carries = [(jnp.full((1, q.shape[0]), NEG_BIG, F32), jnp.zeros((VT_ROWS, q.shape[0]), F32)) for q in qs]
    state = (carries, issue(0, s_refs[0]))
    looped = _looped_blocks(n_blk - 1, 2)
    if looped:
        def body(t, state):
            return stage(2 * t + 1, 1, stage(2 * t, 0, state))
        state = lax.fori_loop(0, looped // 2, body, state)
    for blk in range(looped, n_blk):
        state = stage(blk, blk % 2, state, issue_next=blk + 1 < n_blk)
    carries, _ = state
    outs = []
    for i in range(n):
        s_meta = jnp.where(_meta_mask(qs[i].shape[0]), scores(i, n_real, META_PAD), NEG_BIG)
        _, acc = _flash_update(s_meta, _col_max(s_meta), vt_at(i, n_real, META_PAD), carries[i])
        outs.append(acc[:HEAD_V] / acc[HEAD_V:HEAD_V + 1])
    return outs


def _mla_kernel(q_ref, k_ref, vt_ref, o_ref, s0_ref, s1_ref, *, n_real, tk):
    qs = [q_ref[0, 0], q_ref[1, 0]]
    k_at = lambda i, s, n: k_ref[i, 0, _lane_ds(s, n), :]
    vt_at = lambda i, s, n: vt_ref[i, :, _lane_ds(s, n)]

    def write(outs):
        o_ref[0] = jnp.concatenate(outs, axis=0).T.astype(BF16)

    _attend(qs, k_at, vt_at, (s0_ref, s1_ref), n_real, tk, write)


def _kv_tile(n_real):
    return _div_tile(n_real, KV_BLOCK, LANE)


def _mla_attn(qa, ka, vta, batch, seq_rows):
    n_real = seq_rows - META_PAD
    tq = _div_tile(seq_rows, 1792, LANE)
    tk = _kv_tile(n_real)
    q4 = qa.reshape(MLA_HEADS, batch, seq_rows, LANE)
    k4 = ka.reshape(MLA_HEADS, batch, seq_rows, LANE)
    return pl.pallas_call(
        functools.partial(_mla_kernel, n_real=n_real, tk=tk),
        grid=(batch, MLA_HEADS // 2, seq_rows // tq),
        in_specs=[
            pl.BlockSpec((2, 1, tq, LANE), lambda b, j, i: (j, b, i, 0)),
            pl.BlockSpec((2, 1, seq_rows, LANE), lambda b, j, i: (j, b, 0, 0)),
            pl.BlockSpec((2, VT_ROWS, seq_rows), lambda b, j, i: (j, 0, b)),
        ],
        out_specs=pl.BlockSpec((1, tq, LANE), lambda b, j, i: (b, i, j)),
        out_shape=jax.ShapeDtypeStruct((batch, seq_rows, MLA_HEADS * MLA_V), BF16),
        scratch_shapes=[pltpu.VMEM((2, tk, tq), F32), pltpu.VMEM((2, tk, tq), F32)],
        compiler_params=_params("parallel", "parallel", "parallel"),
        name="mla_attn",
    )(q4, k4, vta).reshape(batch * seq_rows, MLA_HEADS * MLA_V)


def _gqa_streams(tq):
    for streams in (GQA_GROUP, 2, 1):
        if (GQA_GROUP // streams * tq) % MXU_DIM == 0:
            return streams
    return 1


def _gqa_kernel(q_ref, k_ref, vt_ref, o_ref, s0_ref, s1_ref, *, n_real, tk):
    g, _, tq, _ = q_ref.shape
    streams = s0_ref.shape[0]
    per = g // streams
    qs = [q_ref[i * per:(i + 1) * per, 0].reshape(per * tq, LANE) for i in range(streams)]
    k_at = lambda i, s, n: k_ref[0, _lane_ds(s, n), :]
    vt_at = lambda i, s, n: vt_ref[0, :, _lane_ds(s, n)]

    def write(outs):
        heads = [o[:, i * tq:(i + 1) * tq] for o in outs for i in range(per)]
        o_ref[0] = jnp.concatenate(heads, axis=0).T.astype(BF16)

    _attend(qs, k_at, vt_at, (s0_ref, s1_ref), n_real, tk, write)


def _gqa_attn(qb, kb, vtb, batch, seq_rows):
    n_real = seq_rows - META_PAD
    tq = _div_tile(seq_rows, 704, LANE // 2)
    streams = _gqa_streams(tq)
    tk = _kv_tile(n_real)
    q4 = qb.reshape(GQA_HEADS, batch, seq_rows, LANE)
    k3 = kb.reshape(batch, seq_rows, LANE)
    width = GQA_GROUP * GQA_HD
    s_buf = pltpu.VMEM((streams, tk, GQA_GROUP // streams * tq), F32)
    return pl.pallas_call(
        functools.partial(_gqa_kernel, n_real=n_real, tk=tk),
        grid=(batch, GQA_KV_HEADS, seq_rows // tq),
        in_specs=[
            pl.BlockSpec((GQA_GROUP, 1, tq, LANE), lambda b, j, i: (j, b, i, 0)),
            pl.BlockSpec((1, seq_rows, LANE), lambda b, j, i: (b, 0, 0)),
            pl.BlockSpec((1, VT_ROWS, seq_rows), lambda b, j, i: (j, 0, b)),
        ],
        out_specs=pl.BlockSpec((1, tq, width), lambda b, j, i: (b, i, j)),
        out_shape=jax.ShapeDtypeStruct((batch, seq_rows, GQA_HEADS * GQA_HD), BF16),
        scratch_shapes=[s_buf, s_buf],
        compiler_params=_params("parallel", "parallel", "parallel"),
        name="gqa_attn",
    )(q4, k3, vtb).reshape(batch * seq_rows, GQA_HEADS * GQA_HD)


def _merge_rows(h_ref, oa_ref, ob_ref, gmix_ref, wg_ref, bg_ref, wao_ref, wbo_ref, wout_ref):
    h = h_ref[...]
    d = h.shape[-1]
    hn = _rms(h, gmix_ref[...]).astype(BF16)
    gates = jax.nn.sigmoid(_dot(hn, wg_ref[...]) + bg_ref[...])
    merged = gates[:, :d] * _dot(oa_ref[...], wao_ref[...]) + gates[:, d:] * _dot(ob_ref[...], wbo_ref[...])
    return h + _dot(merged.astype(BF16), wout_ref[...])


def _merge_kernel(*refs):
    refs[-1][...] = _merge_rows(*refs[:-1])


def _merge_ffn_kernel(*refs, chunks):
    g_ref, w13_ref, w2_ref, o_ref = refs[-4:]
    o_ref[...] = _swiglu_residual(_merge_rows(*refs[:-4]), g_ref, w13_ref, w2_ref, chunks)


def _merge(h, oa, ob, lw, ffn=None):
    rows, d = h.shape
    tm = _div_tile(rows, ROW_TILE if ffn is None else ROW_TILE // 2, LANE)
    row_spec = lambda w: pl.BlockSpec((tm, w), lambda i: (i, 0))
    consts = [lw["g_mix"], lw["w_gates"], lw["b_gate"], lw["w_mla_o"], lw["w_gqa_o"], lw["w_out"]]
    kern = _merge_kernel
    if ffn is not None:
        consts += list(ffn[:3])
        kern = functools.partial(_merge_ffn_kernel, chunks=ffn[3])
    return pl.pallas_call(
        kern,
        grid=(rows // tm,),
        in_specs=[row_spec(d), row_spec(oa.shape[1]), row_spec(ob.shape[1])]
        + [_const_spec(c.shape) for c in consts],
        out_specs=row_spec(d),
        out_shape=jax.ShapeDtypeStruct((rows, d), F32),
        compiler_params=_params("parallel"),
        name="merge" if ffn is None else "merge_ffn",
    )(h, oa, ob, *consts)


def _block_diag_mean(groups):
    assert sum(groups) == MXU_DIM
    m = np.zeros((MXU_DIM, MXU_DIM), np.float32)
    c0 = 0
    for gsz in groups:
        m[c0:c0 + gsz, c0:c0 + gsz] = 1.0 / gsz
        c0 += gsz
    return jnp.asarray(m, BF16)


def _rotate_half_swap():
    lanes = np.arange(MXU_DIM)
    src = np.where(lanes % (2 * ROPE_PAIR) < ROPE_PAIR, lanes + ROPE_PAIR, lanes - ROPE_PAIR)
    m = np.zeros((MXU_DIM, MXU_DIM), np.float32)
    m[src, lanes] = 1.0
    return jnp.asarray(m, BF16)


def _rope_angles(pos, dim):
    inv = 1.0 / (ROPE_THETA ** (jnp.arange(0, dim, 2, dtype=jnp.float32) / dim))
    return pos.astype(jnp.float32)[:, None] * inv[None, :]


def _tables(n_real):
    p = jnp.arange(n_real + META_PAD, dtype=jnp.int32)
    real = p < n_real
    meta = (p >= n_real) & (p < n_real + N_META)
    pos_1d = jnp.where(real, p + N_META, jnp.where(meta, p - n_real, 0))
    row = jnp.where(real, p // GRID_W, 0)
    col = jnp.where(real, p % GRID_W, 0)
    a1, ar, ac = _rope_angles(pos_1d, MLA_ROPE), _rope_angles(row, AXIAL_HALF), _rope_angles(col, AXIAL_HALF)
    ones = jnp.ones((p.shape[0], MLA_NOPE), F32)
    pad1 = jnp.ones((p.shape[0], LANE - MLA_QK), F32)
    cos_a = jnp.concatenate([ones, jnp.cos(a1), jnp.cos(a1), pad1], axis=1)
    sin_a = jnp.concatenate([0 * ones, -jnp.sin(a1), jnp.sin(a1), 0 * pad1], axis=1)
    cos_b = jnp.tile(jnp.concatenate([jnp.cos(ar), jnp.cos(ar), jnp.cos(ac), jnp.cos(ac)], axis=1), (1, 2))
    sin_b = jnp.tile(jnp.concatenate([-jnp.sin(ar), jnp.sin(ar), -jnp.sin(ac), jnp.sin(ac)], axis=1), (1, 2))
    return dict(
        cos_a=cos_a, sin_a=sin_a, cos_b=cos_b, sin_b=sin_b,
        bd_a=_block_diag_mean([MLA_NOPE, MLA_ROPE, LANE - MLA_QK] * 2),
        bd_c=_block_diag_mean([GQA_HD] * 4),
        bd_m=_block_diag_mean([MLA_NOPE, MLA_ROPE, LANE - MLA_QK, GQA_HD, GQA_HD]),
        swap=_rotate_half_swap(),
    )


def _prep_layer(ffn1_norm, ffn1_w13, ffn1_w2, mix_norm, w_in, b_gate, q_a_norm, kv_a_norm, w_uq, w_ukv,
                mla_q_norm, mla_k_norm, gqa_q_norm, gqa_k_norm, w_mla_o, w_gqa_o, w_out,
                ffn2_norm, ffn2_w13, ffn2_w2, chunks):
    d = w_in.shape[0]
    d_ff = ffn1_w2.shape[0]

    def w13r(w13):
        pieces = []
        for c0, w in chunks:
            pieces += [w13[:, c0:c0 + w], w13[:, d_ff + c0:d_ff + c0 + w]]
        return jnp.concatenate(pieces, axis=1).astype(BF16)

    cuts = np.cumsum([0, Q_LORA, KV_LORA, MLA_ROPE, GQA_HEADS * GQA_HD, GQA_KV_HEADS * GQA_HD,
                      GQA_KV_HEADS * GQA_HD]).tolist()
    w_cq, w_ckv, w_kr, w_qg, w_kg, w_vg = (w_in[:, cuts[i]:cuts[i + 1]] for i in range(6))
    w_gates = w_in[:, cuts[6]:]
    kr_cols = [jnp.zeros((d, MLA_NOPE), F32), w_kr, jnp.zeros((d, LANE - MLA_QK), F32)]
    w_in_s = jnp.concatenate([w_qg] + kr_cols + [w_kg, w_vg, w_cq, w_ckv], axis=1).astype(BF16)
    assert w_in_s.shape[1] == _C_END

    uq_cols, ukv_cols, v_cols = [], [], []
    for hd in range(MLA_HEADS):
        uq_cols += [w_uq[:, hd * MLA_QK:(hd + 1) * MLA_QK], jnp.zeros((Q_LORA, LANE - MLA_QK), F32)]
        base = hd * (MLA_NOPE + MLA_V)
        ukv_cols += [w_ukv[:, base:base + MLA_NOPE]]
        v_cols += [w_ukv[:, base + MLA_NOPE:base + MLA_NOPE + MLA_V]]
    w_uq_p = jnp.concatenate(uq_cols, axis=1).astype(BF16)
    w_ukv_p = jnp.concatenate(ukv_cols + v_cols, axis=1).astype(BF16)

    zq = jnp.zeros((LANE - MLA_QK,), F32)
    ga_q = jnp.tile(jnp.concatenate([mla_q_norm, zq]), MLA_HEADS) * (MLA_QK ** -0.5 * LOG2E)
    ga_k = jnp.tile(mla_k_norm[:MLA_NOPE], MLA_HEADS)
    ga_kr = jnp.concatenate([jnp.zeros((MLA_NOPE,), F32), mla_k_norm[MLA_NOPE:], zq])
    gb_q = jnp.tile(gqa_q_norm, GQA_HEADS) * (GQA_HD ** -0.5 * LOG2E)
    gb_k = jnp.tile(gqa_k_norm, GQA_KV_HEADS)
    row = lambda v: v.reshape(1, -1).astype(F32)
    return dict(
        g_ffn1=row(ffn1_norm), w13r_1=w13r(ffn1_w13), w2_1=ffn1_w2.astype(BF16),
        g_ffn2=row(ffn2_norm), w13r_2=w13r(ffn2_w13), w2_2=ffn2_w2.astype(BF16),
        g_mix=row(mix_norm), w_in_s=w_in_s, g_q=row(q_a_norm), g_kv=row(kv_a_norm),
        w_uq_p=w_uq_p, w_ukv_p=w_ukv_p,
        ga_q=row(ga_q), ga_k=row(ga_k), ga_kr=row(ga_kr), gb_q=row(gb_q), gb_k=row(gb_k),
        w_gates=w_gates.astype(BF16), b_gate=row(b_gate),
        w_mla_o=w_mla_o.astype(BF16), w_gqa_o=w_gqa_o.astype(BF16), w_out=w_out.astype(BF16),
    )


def _trunk(x, meta_tokens, layers, chunks):
    batch, n_real, d = x.shape
    assert n_real % GRID_W == 0 and n_real % (8 * LANE) == 0
    seq_rows = n_real + META_PAD
    tail = jnp.concatenate([meta_tokens.astype(x.dtype), jnp.zeros((META_PAD - N_META, d), x.dtype)], axis=0)
    h = jnp.concatenate([x, jnp.broadcast_to(tail[None], (batch, META_PAD, d))], axis=1)
    h = h.reshape(batch * seq_rows, d)
    tabs = _tables(n_real)
    for idx, lw in enumerate(layers):
        h = _ffn(h, lw["g_ffn1"], lw["w13r_1"], lw["w2_1"], chunks)
        qa, ka, vta, qb, kb, vtb = _proj(h, lw, tabs, seq_rows)
        oa = _mla_attn(qa, ka, vta, batch, seq_rows)
        ob = _gqa_attn(qb, kb, vtb, batch, seq_rows)
        if idx < len(layers) - 1:
            h = _merge(h, oa, ob, lw, ffn=(lw["g_ffn2"], lw["w13r_2"], lw["w2_2"], chunks))
        else:
            h = _merge(h, oa, ob, lw)
            h = _ffn(h, lw["g_ffn2"], lw["w13r_2"], lw["w2_2"], chunks, (batch, seq_rows, n_real))
    return h


def kernel(x_prompt, x_sample, meta_tokens, ffn1_norm, ffn1_w13, ffn1_w2, mix_norm, w_in, b_gate, q_a_norm,
           kv_a_norm, w_uq, w_ukv, mla_q_norm, mla_k_norm, gqa_q_norm, gqa_k_norm, w_mla_o, w_gqa_o, w_out,
           ffn2_norm, ffn2_w13, ffn2_w2):
    per_layer = (ffn1_norm, ffn1_w13, ffn1_w2, mix_norm, w_in, b_gate, q_a_norm, kv_a_norm, w_uq, w_ukv,
                 mla_q_norm, mla_k_norm, gqa_q_norm, gqa_k_norm, w_mla_o, w_gqa_o, w_out,
                 ffn2_norm, ffn2_w13, ffn2_w2)
    chunks = _ff_chunks(ffn1_w2.shape[1])
    layers = [_prep_layer(*(p[l] for p in per_layer), chunks=chunks) for l in range(ffn1_norm.shape[0])]
    return (_trunk(x_prompt, meta_tokens, layers, chunks), _trunk(x_sample, meta_tokens, layers, chunks))
```

```python
import functools
import math

import numpy as np
import jax
import jax.numpy as jnp
from jax import lax
from jax.experimental import pallas as pl
from jax.experimental.pallas import tpu as pltpu

N_META = 16
GRID_W = 64
EPS = 1e-6
ROPE_THETA = 10000.0
MLA_HEADS = 8
MLA_NOPE = 64
MLA_ROPE = 32
MLA_QK = MLA_NOPE + MLA_ROPE
MLA_V = 64
Q_LORA = 384
KV_LORA = 256
GQA_HEADS = 8
GQA_KV_HEADS = 2
GQA_GROUP = GQA_HEADS // GQA_KV_HEADS
GQA_HD = 64
AXIAL_HALF = GQA_HD // 2
ROPE_PAIR = 16
HEAD_V = 64

LANE = 128
MXU_DIM = 256
BF16_ROWS = 16
VMEM_LIMIT = 56 * 1024 * 1024

META_PAD = LANE
VT_ROWS = HEAD_V + BF16_ROWS
KV_BLOCK = 512
ATTN_UNROLL = 4
MAX_RISE = 24.0
ROW_TILE = 1024
LOG2E = math.log2(math.e)
NEG_BIG = -1e30

BF16 = jnp.bfloat16
F32 = jnp.float32


def _div_tile(n, target, mult):
    best = None
    for t in range(mult, min(n, target) + 1, mult):
        if n % t == 0:
            best = t
    assert best is not None, (n, target, mult)
    return best


def _ff_chunks(d_ff):
    chunks, c0 = [], 0
    while c0 < d_ff:
        w = min(3 * MXU_DIM, d_ff - c0)
        chunks.append((c0, w))
        c0 += w
    return tuple(chunks)


def _dot(a, b):
    return jnp.dot(a, b, preferred_element_type=F32)


def _rms(x, g):
    return x * lax.rsqrt(jnp.mean(x * x, axis=-1, keepdims=True) + EPS) * g


def _const_spec(shape):
    nd = len(shape)
    return pl.BlockSpec(shape, lambda *_: (0,) * nd, pipeline_mode=pl.Buffered(1))


def _params(*sem):
    return pltpu.CompilerParams(dimension_semantics=sem, vmem_limit_bytes=VMEM_LIMIT)


def _ffn_kernel(h_ref, g_ref, w13_ref, w2_ref, o_ref, *, chunks):
    h = h_ref[...]
    xn = _rms(h, g_ref[...]).astype(BF16)
    acc = None
    off = 0
    for c0, w in chunks:
        gu = _dot(xn, w13_ref[:, off:off + 2 * w])
        gate, up = gu[:, :w], gu[:, w:]
        a = (gate * jax.nn.sigmoid(gate) * up).astype(BF16)
        part = _dot(a, w2_ref[c0:c0 + w, :])
        acc = part if acc is None else acc + part
        off += 2 * w
    o_ref[...] = h + 0.5 * acc


def _ffn(h, g, w13r, w2, chunks, real_only=None):
    rows, d = h.shape
    if real_only is None:
        tm = _div_tile(rows, ROW_TILE, LANE)
        grid, sem = (rows // tm,), ("parallel",)
        spec = pl.BlockSpec((tm, d), lambda i: (i, 0))
        out_shape = jax.ShapeDtypeStruct((rows, d), F32)
    else:
        batch, seq_rows, n_real = real_only
        tm = _div_tile(n_real, ROW_TILE, LANE)
        h = h.reshape(batch, seq_rows, d)
        grid, sem = (batch, n_real // tm), ("parallel", "parallel")
        spec = pl.BlockSpec((None, tm, d), lambda b, i: (b, i, 0))
        out_shape = jax.ShapeDtypeStruct((batch, n_real, d), F32)
    return pl.pallas_call(
        functools.partial(_ffn_kernel, chunks=chunks),
        grid=grid,
        in_specs=[spec, _const_spec(g.shape), _const_spec(w13r.shape), _const_spec(w2.shape)],
        out_specs=spec,
        out_shape=out_shape,
        compiler_params=_params(*sem),
        name="ffn",
    )(h, g, w13r, w2)


_C_QG = 0
_C_KR = 512
_C_KG = 640
_C_VG = 768
_C_CQ = 896
_C_CKV = 1280
_C_END = 1536


def _group_ms(x, bd):
    return _dot((x * x).astype(BF16), bd)


def _rope(y, cos, sin, swap):
    return y * cos + _dot(y.astype(BF16), swap) * sin


def _store_vt(vt_ref, head0, v_pair, ones_tile):
    vt = v_pair.T
    for k in range(2):
        vt_ref[head0 + k, :HEAD_V, :] = vt[k * HEAD_V:(k + 1) * HEAD_V].astype(BF16)
        vt_ref[head0 + k, HEAD_V:, :] = ones_tile


def _proj_kernel(h_ref, gmix_ref, win_ref, gq_ref, gkv_ref, wuq_ref, wukv_ref,
                 bd_a_ref, bd_c_ref, bd_m_ref, swap_ref, gaq_ref, gak_ref, gakr_ref, gbq_ref, gbk_ref,
                 cos_a_ref, sin_a_ref, cos_b_ref, sin_b_ref,
                 qa_ref, ka_ref, vta_ref, qb_ref, kb_ref, vtb_ref):
    tm = h_ref.shape[0]
    hn = _rms(h_ref[...], gmix_ref[...]).astype(BF16)
    z = _dot(hn, win_ref[...])
    cq = _rms(z[:, _C_CQ:_C_CKV], gq_ref[...]).astype(BF16)
    ckv = _rms(z[:, _C_CKV:_C_END], gkv_ref[...]).astype(BF16)
    qa = _dot(cq, wuq_ref[...])
    kva = _dot(ckv, wukv_ref[...])
    n_kn = MLA_HEADS * MLA_NOPE

    low_half = lax.broadcasted_iota(jnp.int32, (tm, LANE), 1) < HEAD_V
    cos_a, sin_a = cos_a_ref[...], sin_a_ref[...]
    cos_b, sin_b = cos_b_ref[...], sin_b_ref[...]
    pair = lambda left, right: jnp.concatenate([left, right], axis=1)
    bd_a, bd_c, bd_m, swap = bd_a_ref[...], bd_c_ref[...], bd_m_ref[...], swap_ref[...]
    ones_tile = (lax.broadcasted_iota(jnp.int32, (BF16_ROWS, tm), 0) == 0).astype(F32).astype(BF16)

    x = z[:, _C_KR:_C_VG]
    y = x * lax.rsqrt(_group_ms(x, bd_m) + EPS) * pair(gakr_ref[...], gbk_ref[...])
    y = _rope(y, pair(cos_a, cos_b), pair(sin_a, sin_b), swap)
    kr = y[:, :LANE]
    kb_ref[...] = y[:, LANE:].astype(BF16)
    _store_vt(vtb_ref, 0, z[:, _C_VG:_C_CQ], ones_tile)
    for c in range(MLA_HEADS // 2):
        v0 = n_kn + c * LANE
        _store_vt(vta_ref, 2 * c, kva[:, v0:v0 + LANE], ones_tile)

    cos_aa, sin_aa = pair(cos_a, cos_a), pair(sin_a, sin_a)
    for c in range(MLA_HEADS * LANE // MXU_DIM):
        sl = slice(c * MXU_DIM, (c + 1) * MXU_DIM)
        x = qa[:, sl]
        y = _rope(x * lax.rsqrt(_group_ms(x, bd_a) + EPS) * gaq_ref[:, sl], cos_aa, sin_aa, swap)
        for half in range(2):
            qa_ref[2 * c + half] = y[:, half * LANE:(half + 1) * LANE].astype(BF16)

    for c in range(n_kn // MXU_DIM):
        sl = slice(c * MXU_DIM, (c + 1) * MXU_DIM)
        xk = kva[:, sl]
        yk = xk * lax.rsqrt(_group_ms(xk, bd_c) + EPS) * gak_ref[:, sl]
        for half in range(2):
            t = yk[:, half * LANE:(half + 1) * LANE]
            for k, tile in enumerate((t, pltpu.roll(t, HEAD_V, 1))):
                ka_ref[4 * c + 2 * half + k] = (jnp.where(low_half, tile, 0.0) + kr).astype(BF16)

    cos_bb, sin_bb = pair(cos_b, cos_b), pair(sin_b, sin_b)
    for kv in range(GQA_KV_HEADS):
        sl = slice(kv * MXU_DIM, (kv + 1) * MXU_DIM)
        xg = z[:, sl]
        yg = _rope(xg * lax.rsqrt(_group_ms(xg, bd_c) + EPS) * gbq_ref[:, sl], cos_bb, sin_bb, swap)
        keep = low_half if kv == 0 else jnp.logical_not(low_half)
        for half in range(2):
            t = yg[:, half * LANE:(half + 1) * LANE]
            tiles = (t, pltpu.roll(t, GQA_HD, 1)) if kv == 0 else (pltpu.roll(t, GQA_HD, 1), t)
            for k in range(2):
                head = GQA_GROUP * kv + 2 * half + k
                qb_ref[head] = jnp.where(keep, tiles[k], 0.0).astype(BF16)


def _proj(h, lw, tabs, seq_rows):
    rows, d = h.shape
    tm = _div_tile(seq_rows, 640, LANE)
    per_seq = seq_rows // tm
    row_spec = lambda w: pl.BlockSpec((tm, w), lambda i: (i, 0))
    tab_spec = pl.BlockSpec((tm, LANE), lambda i: (i % per_seq, 0))
    head_spec = pl.BlockSpec((MLA_HEADS, tm, LANE), lambda i: (0, i, 0))
    vt_spec = lambda n: pl.BlockSpec((n, VT_ROWS, tm), lambda i: (0, 0, i))
    consts = [lw["g_mix"], lw["w_in_s"], lw["g_q"], lw["g_kv"], lw["w_uq_p"], lw["w_ukv_p"],
              tabs["bd_a"], tabs["bd_c"], tabs["bd_m"], tabs["swap"],
              lw["ga_q"], lw["ga_k"], lw["ga_kr"], lw["gb_q"], lw["gb_k"]]
    heads = jax.ShapeDtypeStruct((MLA_HEADS, rows, LANE), BF16)
    return pl.pallas_call(
        _proj_kernel,
        grid=(rows // tm,),
        in_specs=[row_spec(d)] + [_const_spec(c.shape) for c in consts] + [tab_spec] * 4,
        out_specs=[head_spec, head_spec, vt_spec(MLA_HEADS), head_spec, row_spec(LANE),
                   vt_spec(GQA_KV_HEADS)],
        out_shape=[heads, heads, jax.ShapeDtypeStruct((MLA_HEADS, VT_ROWS, rows), BF16), heads,
                   jax.ShapeDtypeStruct((rows, LANE), BF16),
                   jax.ShapeDtypeStruct((GQA_KV_HEADS, VT_ROWS, rows), BF16)],
        compiler_params=_params("parallel"),
        name="proj",
    )(h, *consts, tabs["cos_a"], tabs["sin_a"], tabs["cos_b"], tabs["sin_b"])


def _lane_ds(start, size):
    return pl.ds(start if isinstance(start, int) else pl.multiple_of(start, LANE), size)


def _col_max(s):
    return jnp.max(s, axis=0, keepdims=True)


def _flash_update(s, s_max, vt, carry):
    m, acc = carry
    m_new = jnp.maximum(m, s_max)
    p = jnp.exp2(s - m_new).astype(BF16)
    acc = jnp.exp2(m - m_new) * acc + _dot(vt, p)
    return m_new, acc


def _scores_fn(qs, k_at):
    def scores(i, start, size):
        return lax.dot_general(k_at(i, start, size), qs[i], (((1,), (1,)), ((), ())),
                               preferred_element_type=F32)
    return scores


def _looped_blocks(n_blocks, unroll):
    return n_blocks // unroll * unroll if n_blocks >= 2 * unroll else 0


def _meta_mask(nq):
    return lax.broadcasted_iota(jnp.int32, (META_PAD, nq), 0) < N_META


def _flash_one_pass(qs, k_at, vt_at, n_real, tk):
    n = len(qs)
    n_blk = n_real // tk
    assert n_real % tk == 0
    scores = _scores_fn(qs, k_at)

    def fold(i, start, size, carry, mask=None):
        m, acc, rise = carry
        s = scores(i, start, size)
        if mask is not None:
            s = jnp.where(mask, s, NEG_BIG)
        s_max = _col_max(s)
        acc = acc + _dot(vt_at(i, start, size), jnp.exp2(s - m).astype(BF16))
        m_new = jnp.maximum(m, s_max)
        return m_new, acc * jnp.exp2(m - m_new), jnp.maximum(rise, s_max - m)

    def sweep(first, count, carries):
        for k in range(count):
            carries = [fold(i, (first + k) * tk, tk, carries[i]) for i in range(n)]
        return carries

    carries = []
    for i in range(n):
        s = jnp.where(_meta_mask(qs[i].shape[0]), scores(i, n_real, META_PAD), NEG_BIG)
        m = _col_max(s)
        carries.append((m, _dot(vt_at(i, n_real, META_PAD), jnp.exp2(s - m).astype(BF16)), jnp.zeros_like(m)))
    looped = _looped_blocks(n_blk - 1, ATTN_UNROLL)
    if looped:
        carries = lax.fori_loop(0, looped // ATTN_UNROLL,
                                lambda t, c: sweep(t * ATTN_UNROLL, ATTN_UNROLL, c), carries)
    carries = sweep(looped, n_blk - looped, carries)
    outs, rise = [], None
    for i in range(n):
        _, acc, r = carries[i]
        outs.append(acc[:HEAD_V] / acc[HEAD_V:HEAD_V + 1])
        rise = jnp.max(r) if rise is None else jnp.maximum(rise, jnp.max(r))
    return outs, rise


def _attend(qs, k_at, vt_at, s_refs, n_real, tk, write):
    outs, rise = _flash_one_pass(qs, k_at, vt_at, n_real, tk)
    write(outs)

    @pl.when(rise > MAX_RISE)
    def _():
        write(_flash(qs, k_at, vt_at, s_refs, n_real, tk))


def _flash(qs, k_at, vt_at, s_refs, n_real, tk):
    n = len(qs)
    n_blk = n_real // tk
    assert n_real % tk == 0
    scores = _scores_fn(qs, k_at)

    def issue(blk, s_buf):
        maxes = []
        for i in range(n):
            s = scores(i, blk * tk, tk)
            s_buf[i] = s
            maxes.append(_col_max(s))
        return maxes

    def stage(blk, parity, state, issue_next=True):
        carries, maxes = state
        next_maxes = issue(blk + 1, s_refs[1 - parity]) if issue_next else maxes
        carries = [_flash_update(s_refs[parity][i], maxes[i], vt_at(i, blk * tk, tk), carries[i])
                   for i in range(n)]
        return carries, next_maxes

    carries = [(jnp.full((1, q.shape[0]), NEG_BIG, F32), jnp.zeros((VT_ROWS, q.shape[0]), F32)) for q in qs]
    state = (carries, issue(0, s_refs[0]))
    looped = _looped_blocks(n_blk - 1, 2)
    if looped:
        def body(t, state):
            return stage(2 * t + 1, 1, stage(2 * t, 0, state))
        state = lax.fori_loop(0, looped // 2, body, state)
    for blk in range(looped, n_blk):
        state = stage(blk, blk % 2, state, issue_next=blk + 1 < n_blk)
    carries, _ = state
    outs = []
    for i in range(n):
        s_meta = jnp.where(_meta_mask(qs[i].shape[0]), scores(i, n_real, META_PAD), NEG_BIG)
        _, acc = _flash_update(s_meta, _col_max(s_meta), vt_at(i, n_real, META_PAD), carries[i])
        outs.append(acc[:HEAD_V] / acc[HEAD_V:HEAD_V + 1])
    return outs


def _mla_kernel(q_ref, k_ref, vt_ref, o_ref, s0_ref, s1_ref, *, n_real, tk):
    qs = [q_ref[0, 0], q_ref[1, 0]]
    k_at = lambda i, s, n: k_ref[i, 0, _lane_ds(s, n), :]
    vt_at = lambda i, s, n: vt_ref[i, :, _lane_ds(s, n)]

    def write(outs):
        o_ref[0] = jnp.concatenate(outs, axis=0).T.astype(BF16)

    _attend(qs, k_at, vt_at, (s0_ref, s1_ref), n_real, tk, write)


def _kv_tile(n_real):
    return _div_tile(n_real, KV_BLOCK, LANE)


def _mla_attn(qa, ka, vta, batch, seq_rows):
    n_real = seq_rows - META_PAD
    tq = _div_tile(seq_rows, 1792, LANE)
    tk = _kv_tile(n_real)
    q4 = qa.reshape(MLA_HEADS, batch, seq_rows, LANE)
    k4 = ka.reshape(MLA_HEADS, batch, seq_rows, LANE)
    return pl.pallas_call(
        functools.partial(_mla_kernel, n_real=n_real, tk=tk),
        grid=(batch, MLA_HEADS // 2, seq_rows // tq),
        in_specs=[
            pl.BlockSpec((2, 1, tq, LANE), lambda b, j, i: (j, b, i, 0)),
            pl.BlockSpec((2, 1, seq_rows, LANE), lambda b, j, i: (j, b, 0, 0)),
            pl.BlockSpec((2, VT_ROWS, seq_rows), lambda b, j, i: (j, 0, b)),
        ],
        out_specs=pl.BlockSpec((1, tq, LANE), lambda b, j, i: (b, i, j)),
        out_shape=jax.ShapeDtypeStruct((batch, seq_rows, MLA_HEADS * MLA_V), BF16),
        scratch_shapes=[pltpu.VMEM((2, tk, tq), F32), pltpu.VMEM((2, tk, tq), F32)],
        compiler_params=_params("parallel", "parallel", "parallel"),
        name="mla_attn",
    )(q4, k4, vta).reshape(batch * seq_rows, MLA_HEADS * MLA_V)


def _gqa_streams(tq):
    for streams in (GQA_GROUP, 2, 1):
        if (GQA_GROUP // streams * tq) % MXU_DIM == 0:
            return streams
    return 1


def _gqa_kernel(q_ref, k_ref, vt_ref, o_ref, s0_ref, s1_ref, *, n_real, tk):
    g, _, tq, _ = q_ref.shape
    streams = s0_ref.shape[0]
    per = g // streams
    qs = [q_ref[i * per:(i + 1) * per, 0].reshape(per * tq, LANE) for i in range(streams)]
    k_at = lambda i, s, n: k_ref[0, _lane_ds(s, n), :]
    vt_at = lambda i, s, n: vt_ref[0, :, _lane_ds(s, n)]

    def write(outs):
        heads = [o[:, i * tq:(i + 1) * tq] for o in outs for i in range(per)]
        o_ref[0] = jnp.concatenate(heads, axis=0).T.astype(BF16)

    _attend(qs, k_at, vt_at, (s0_ref, s1_ref), n_real, tk, write)


def _gqa_attn(qb, kb, vtb, batch, seq_rows):
    n_real = seq_rows - META_PAD
    tq = _div_tile(seq_rows, 704, LANE // 2)
    streams = _gqa_streams(tq)
    tk = _kv_tile(n_real)
    q4 = qb.reshape(GQA_HEADS, batch, seq_rows, LANE)
    k3 = kb.reshape(batch, seq_rows, LANE)
    width = GQA_GROUP * GQA_HD
    s_buf = pltpu.VMEM((streams, tk, GQA_GROUP // streams * tq), F32)
    return pl.pallas_call(
        functools.partial(_gqa_kernel, n_real=n_real, tk=tk),
        grid=(batch, GQA_KV_HEADS, seq_rows // tq),
        in_specs=[
            pl.BlockSpec((GQA_GROUP, 1, tq, LANE), lambda b, j, i: (j, b, i, 0)),
            pl.BlockSpec((1, seq_rows, LANE), lambda b, j, i: (b, 0, 0)),
            pl.BlockSpec((1, VT_ROWS, seq_rows), lambda b, j, i: (j, 0, b)),
        ],
        out_specs=pl.BlockSpec((1, tq, width), lambda b, j, i: (b, i, j)),
        out_shape=jax.ShapeDtypeStruct((batch, seq_rows, GQA_HEADS * GQA_HD), BF16),
        scratch_shapes=[s_buf, s_buf],
        compiler_params=_params("parallel", "parallel", "parallel"),
        name="gqa_attn",
    )(q4, k3, vtb).reshape(batch * seq_rows, GQA_HEADS * GQA_HD)


def _merge_kernel(h_ref, oa_ref, ob_ref, gmix_ref, wg_ref, bg_ref, wao_ref, wbo_ref, wout_ref, o_ref):
    h = h_ref[...]
    d = h.shape[-1]
    hn = _rms(h, gmix_ref[...]).astype(BF16)
    gates = jax.nn.sigmoid(_dot(hn, wg_ref[...]) + bg_ref[...])
    merged = gates[:, :d] * _dot(oa_ref[...], wao_ref[...]) + gates[:, d:] * _dot(ob_ref[...], wbo_ref[...])
    o_ref[...] = h + _dot(merged.astype(BF16), wout_ref[...])


def _merge(h, oa, ob, lw):
    rows, d = h.shape
    tm = _div_tile(rows, ROW_TILE, LANE)
    row_spec = lambda w: pl.BlockSpec((tm, w), lambda i: (i, 0))
    consts = [lw["g_mix"], lw["w_gates"], lw["b_gate"], lw["w_mla_o"], lw["w_gqa_o"], lw["w_out"]]
    return pl.pallas_call(
        _merge_kernel,
        grid=(rows // tm,),
        in_specs=[row_spec(d), row_spec(oa.shape[1]), row_spec(ob.shape[1])]
        + [_const_spec(c.shape) for c in consts],
        out_specs=row_spec(d),
        out_shape=jax.ShapeDtypeStruct((rows, d), F32),
        compiler_params=_params("parallel"),
        name="merge",
    )(h, oa, ob, *consts)


def _block_diag_mean(groups):
    assert sum(groups) == MXU_DIM
    m = np.zeros((MXU_DIM, MXU_DIM), np.float32)
    c0 = 0
    for gsz in groups:
        m[c0:c0 + gsz, c0:c0 + gsz] = 1.0 / gsz
        c0 += gsz
    return jnp.asarray(m, BF16)


def _rotate_half_swap():
    lanes = np.arange(MXU_DIM)
    src = np.where(lanes % (2 * ROPE_PAIR) < ROPE_PAIR, lanes + ROPE_PAIR, lanes - ROPE_PAIR)
    m = np.zeros((MXU_DIM, MXU_DIM), np.float32)
    m[src, lanes] = 1.0
    return jnp.asarray(m, BF16)


def _rope_angles(pos, dim):
    inv = 1.0 / (ROPE_THETA ** (jnp.arange(0, dim, 2, dtype=jnp.float32) / dim))
    return pos.astype(jnp.float32)[:, None] * inv[None, :]


def _tables(n_real):
    p = jnp.arange(n_real + META_PAD, dtype=jnp.int32)
    real = p < n_real
    meta = (p >= n_real) & (p < n_real + N_META)
    pos_1d = jnp.where(real, p + N_META, jnp.where(meta, p - n_real, 0))
    row = jnp.where(real, p // GRID_W, 0)
    col = jnp.where(real, p % GRID_W, 0)
    a1, ar, ac = _rope_angles(pos_1d, MLA_ROPE), _rope_angles(row, AXIAL_HALF), _rope_angles(col, AXIAL_HALF)
    ones = jnp.ones((p.shape[0], MLA_NOPE), F32)
    pad1 = jnp.ones((p.shape[0], LANE - MLA_QK), F32)
    cos_a = jnp.concatenate([ones, jnp.cos(a1), jnp.cos(a1), pad1], axis=1)
    sin_a = jnp.concatenate([0 * ones, -jnp.sin(a1), jnp.sin(a1), 0 * pad1], axis=1)
    cos_b = jnp.tile(jnp.concatenate([jnp.cos(ar), jnp.cos(ar), jnp.cos(ac), jnp.cos(ac)], axis=1), (1, 2))
    sin_b = jnp.tile(jnp.concatenate([-jnp.sin(ar), jnp.sin(ar), -jnp.sin(ac), jnp.sin(ac)], axis=1), (1, 2))
    return dict(
        cos_a=cos_a, sin_a=sin_a, cos_b=cos_b, sin_b=sin_b,
        bd_a=_block_diag_mean([MLA_NOPE, MLA_ROPE, LANE - MLA_QK] * 2),
        bd_c=_block_diag_mean([GQA_HD] * 4),
        bd_m=_block_diag_mean([MLA_NOPE, MLA_ROPE, LANE - MLA_QK, GQA_HD, GQA_HD]),
        swap=_rotate_half_swap(),
    )


def _prep_layer(ffn1_norm, ffn1_w13, ffn1_w2, mix_norm, w_in, b_gate, q_a_norm, kv_a_norm, w_uq, w_ukv,
                mla_q_norm, mla_k_norm, gqa_q_norm, gqa_k_norm, w_mla_o, w_gqa_o, w_out,
                ffn2_norm, ffn2_w13, ffn2_w2, chunks):
    d = w_in.shape[0]
    d_ff = ffn1_w2.shape[0]

    def w13r(w13):
        pieces = []
        for c0, w in chunks:
            pieces += [w13[:, c0:c0 + w], w13[:, d_ff + c0:d_ff + c0 + w]]
        return jnp.concatenate(pieces, axis=1).astype(BF16)

    cuts = np.cumsum([0, Q_LORA, KV_LORA, MLA_ROPE, GQA_HEADS * GQA_HD, GQA_KV_HEADS * GQA_HD,
                      GQA_KV_HEADS * GQA_HD]).tolist()
    w_cq, w_ckv, w_kr, w_qg, w_kg, w_vg = (w_in[:, cuts[i]:cuts[i + 1]] for i in range(6))
    w_gates = w_in[:, cuts[6]:]
    kr_cols = [jnp.zeros((d, MLA_NOPE), F32), w_kr, jnp.zeros((d, LANE - MLA_QK), F32)]
    w_in_s = jnp.concatenate([w_qg] + kr_cols + [w_kg, w_vg, w_cq, w_ckv], axis=1).astype(BF16)
    assert w_in_s.shape[1] == _C_END

    uq_cols, ukv_cols, v_cols = [], [], []
    for hd in range(MLA_HEADS):
        uq_cols += [w_uq[:, hd * MLA_QK:(hd + 1) * MLA_QK], jnp.zeros((Q_LORA, LANE - MLA_QK), F32)]
        base = hd * (MLA_NOPE + MLA_V)
        ukv_cols += [w_ukv[:, base:base + MLA_NOPE]]
        v_cols += [w_ukv[:, base + MLA_NOPE:base + MLA_NOPE + MLA_V]]
    w_uq_p = jnp.concatenate(uq_cols, axis=1).astype(BF16)
    w_ukv_p = jnp.concatenate(ukv_cols + v_cols, axis=1).astype(BF16)

    zq = jnp.zeros((LANE - MLA_QK,), F32)
    ga_q = jnp.tile(jnp.concatenate([mla_q_norm, zq]), MLA_HEADS) * (MLA_QK ** -0.5 * LOG2E)
    ga_k = jnp.tile(mla_k_norm[:MLA_NOPE], MLA_HEADS)
    ga_kr = jnp.concatenate([jnp.zeros((MLA_NOPE,), F32), mla_k_norm[MLA_NOPE:], zq])
    gb_q = jnp.tile(gqa_q_norm, GQA_HEADS) * (GQA_HD ** -0.5 * LOG2E)
    gb_k = jnp.tile(gqa_k_norm, GQA_KV_HEADS)
    row = lambda v: v.reshape(1, -1).astype(F32)
    return dict(
        g_ffn1=row(ffn1_norm), w13r_1=w13r(ffn1_w13), w2_1=ffn1_w2.astype(BF16),
        g_ffn2=row(ffn2_norm), w13r_2=w13r(ffn2_w13), w2_2=ffn2_w2.astype(BF16),
        g_mix=row(mix_norm), w_in_s=w_in_s, g_q=row(q_a_norm), g_kv=row(kv_a_norm),
        w_uq_p=w_uq_p, w_ukv_p=w_ukv_p,
        ga_q=row(ga_q), ga_k=row(ga_k), ga_kr=row(ga_kr), gb_q=row(gb_q), gb_k=row(gb_k),
        w_gates=w_gates.astype(BF16), b_gate=row(b_gate),
        w_mla_o=w_mla_o.astype(BF16), w_gqa_o=w_gqa_o.astype(BF16), w_out=w_out.astype(BF16),
    )


def _trunk(x, meta_tokens, layers, chunks):
    batch, n_real, d = x.shape
    assert n_real % GRID_W == 0 and n_real % (8 * LANE) == 0
    seq_rows = n_real + META_PAD
    tail = jnp.concatenate([meta_tokens.astype(x.dtype), jnp.zeros((META_PAD - N_META, d), x.dtype)], axis=0)
    h = jnp.concatenate([x, jnp.broadcast_to(tail[None], (batch, META_PAD, d))], axis=1)
    h = h.reshape(batch * seq_rows, d)
    tabs = _tables(n_real)
    for idx, lw in enumerate(layers):
        h = _ffn(h, lw["g_ffn1"], lw["w13r_1"], lw["w2_1"], chunks)
        qa, ka, vta, qb, kb, vtb = _proj(h, lw, tabs, seq_rows)
        oa = _mla_attn(qa, ka, vta, batch, seq_rows)
        ob = _gqa_attn(qb, kb, vtb, batch, seq_rows)
        h = _merge(h, oa, ob, lw)
        real_only = (batch, seq_rows, n_real) if idx == len(layers) - 1 else None
        h = _ffn(h, lw["g_ffn2"], lw["w13r_2"], lw["w2_2"], chunks, real_only)
    return h


def kernel(x_prompt, x_sample, meta_tokens, ffn1_norm, ffn1_w13, ffn1_w2, mix_norm, w_in, b_gate, q_a_norm,
           kv_a_norm, w_uq, w_ukv, mla_q_norm, mla_k_norm, gqa_q_norm, gqa_k_norm, w_mla_o, w_gqa_o, w_out,
           ffn2_norm, ffn2_w13, ffn2_w2):
    per_layer = (ffn1_norm, ffn1_w13, ffn1_w2, mix_norm, w_in, b_gate, q_a_norm, kv_a_norm, w_uq, w_ukv,
                 mla_q_norm, mla_k_norm, gqa_q_norm, gqa_k_norm, w_mla_o, w_gqa_o, w_out,
                 ffn2_norm, ffn2_w13, ffn2_w2)
    chunks = _ff_chunks(ffn1_w2.shape[1])
    layers = [_prep_layer(*(p[l] for p in per_layer), chunks=chunks) for l in range(ffn1_norm.shape[0])]
    return (_trunk(x_prompt, meta_tokens, layers, chunks), _trunk(x_sample, meta_tokens, layers, chunks))
```
